```python
import math
import jax, jax.numpy as jnp
from jax import lax
import numpy as np

D_MODEL = 1024
BATCH = 4
SEQ = 8192
DEPTH = 4

N_MIXERS = 2
RMS_EPS = 1e-6
DN_HEAD_DIM = 128
DN_HEADS = D_MODEL // DN_HEAD_DIM
DN_WIDTH = DN_HEADS * DN_HEAD_DIM
DN_CONV = 4
DN_CHUNK = 64
DN_IN = 4 * DN_WIDTH + 2 * DN_HEADS
SWA_HEAD_DIM = 64
SWA_HEADS = D_MODEL // SWA_HEAD_DIM
SWA_KV_HEADS = SWA_HEADS // 8
SWA_GROUP = SWA_HEADS // SWA_KV_HEADS
SWA_QKV = (SWA_HEADS + 2 * SWA_KV_HEADS) * SWA_HEAD_DIM
WINDOW = 128
SWA_BLOCK = WINDOW
NUM_BUCKETS = 32
MAX_DISTANCE = 128
D_FF = 4 * D_MODEL
N_DN_LAYERS = (DEPTH + N_MIXERS - 1) // N_MIXERS
N_SWA_LAYERS = DEPTH // N_MIXERS

kernel_name = 'hybrid_gdn_swa_sink_t5_sqrelu'


def rms_norm(x, w):
    xf = x.astype(jnp.float32)
    y = xf * lax.rsqrt(jnp.mean(xf * xf, axis=-1, keepdims=True) + RMS_EPS)
    return (y * w.astype(jnp.float32)).astype(x.dtype)


def l2_normalize(t):
    return t * lax.rsqrt(jnp.sum(t * t, axis=-1, keepdims=True) + RMS_EPS)


def causal_depthwise_conv(x, w):
    k = w.shape[0]
    return lax.conv_general_dilated(
        x, w[:, None, :].astype(x.dtype), window_strides=(1,), padding=[(k - 1, 0)],
        dimension_numbers=('NWC', 'WIO', 'NWC'), feature_group_count=x.shape[-1])


def chunk_gated_delta_rule(q, k, v, beta, g):
    B, S, H, DK = q.shape
    DV = v.shape[-1]
    C = DN_CHUNK
    N = S // C

    def to_chunks(t):
        return t.reshape(B, N, C, H, -1).transpose(0, 3, 1, 2, 4)

    q = to_chunks(q) * (DK ** -0.5)
    k = to_chunks(k)
    v = to_chunks(v)
    beta = beta.reshape(B, N, C, H).transpose(0, 3, 1, 2)
    gc = jnp.cumsum(g.reshape(B, N, C, H).transpose(0, 3, 1, 2), axis=-1)
    causal = jnp.tril(jnp.ones((C, C), dtype=bool))
    strict = jnp.tril(jnp.ones((C, C), dtype=bool), -1)
    decay = jnp.exp(jnp.where(causal, gc[..., :, None] - gc[..., None, :], -jnp.inf))
    k_beta = k * beta[..., None]
    lower = jnp.where(strict, jnp.einsum('bhnid,bhnjd->bhnij', k_beta, k) * decay, 0.0)
    eye = jnp.eye(C, dtype=jnp.float32)
    rhs = jnp.concatenate([v * beta[..., None], k_beta * jnp.exp(gc)[..., None]], axis=-1)
    sol = lax.linalg.triangular_solve(lower + eye, rhs, left_side=True, lower=True, unit_diagonal=True)
    u, w = sol[..., :DV], sol[..., DV:]
    attn = jnp.einsum('bhnid,bhnjd->bhnij', q, k) * decay
    q_dec = q * jnp.exp(gc)[..., None]
    k_dec = k * jnp.exp(gc[..., -1:] - gc)[..., None]
    g_last = jnp.exp(gc[..., -1])
    xs = tuple(jnp.moveaxis(t, 2, 0) for t in (q_dec, k_dec, u, w, attn, g_last))

    def step(state, inp):
        q_c, k_c, u_c, w_c, a_c, gl = inp
        v_new = u_c - jnp.einsum('bhck,bhkv->bhcv', w_c, state)
        o_c = jnp.einsum('bhck,bhkv->bhcv', q_c, state) + jnp.einsum('bhij,bhjv->bhiv', a_c, v_new)
        state = state * gl[..., None, None] + jnp.einsum('bhck,bhcv->bhkv', k_c, v_new)
        return state, o_c

    s0 = jnp.zeros((B, H, DK, DV), jnp.float32)
    _, o = lax.scan(step, s0, xs)
    return o.transpose(1, 0, 3, 2, 4).reshape(B, S, H, DV)


def gated_deltanet(h, w_in, conv_w, a_log, dt_bias, norm_w, w_out):
    B, S, _ = h.shape
    proj = h @ w_in
    qkv = jax.nn.silu(causal_depthwise_conv(proj[..., :3 * DN_WIDTH], conv_w))
    z = proj[..., 3 * DN_WIDTH:4 * DN_WIDTH]
    b = proj[..., 4 * DN_WIDTH:4 * DN_WIDTH + DN_HEADS]
    a = proj[..., 4 * DN_WIDTH + DN_HEADS:]
    q, k, v = jnp.split(qkv.astype(jnp.float32), 3, axis=-1)
    q = l2_normalize(q.reshape(B, S, DN_HEADS, DN_HEAD_DIM))
    k = l2_normalize(k.reshape(B, S, DN_HEADS, DN_HEAD_DIM))
    v = v.reshape(B, S, DN_HEADS, DN_HEAD_DIM)
    beta = jax.nn.sigmoid(b.astype(jnp.float32))
    g = -jnp.exp(a_log.astype(jnp.float32)) * jax.nn.softplus(a.astype(jnp.float32) + dt_bias.astype(jnp.float32))
    o = chunk_gated_delta_rule(q, k, v, beta, g)
    zf = z.reshape(B, S, DN_HEADS, DN_HEAD_DIM).astype(jnp.float32)
    o = o * lax.rsqrt(jnp.mean(o * o, axis=-1, keepdims=True) + RMS_EPS) * norm_w.astype(jnp.float32) * jax.nn.silu(zf)
    return o.reshape(B, S, DN_WIDTH).astype(h.dtype) @ w_out


def t5_causal_bucket(dist):
    n = jnp.maximum(dist, 0)
    max_exact = NUM_BUCKETS // 2
    large = max_exact + (jnp.log(jnp.maximum(n, 1).astype(jnp.float32) / max_exact)
                         / math.log(MAX_DISTANCE / max_exact) * (NUM_BUCKETS - max_exact)).astype(jnp.int32)
    large = jnp.minimum(large, NUM_BUCKETS - 1)
    return jnp.where(n < max_exact, n, large)


def sliding_window_sink_attention(h, w_qkv, b_qkv, sinks, w_out, b_out, rel_bias):
    B, S, _ = h.shape
    NB = S // SWA_BLOCK
    qkv = h @ w_qkv + b_qkv
    q = qkv[..., :SWA_HEADS * SWA_HEAD_DIM].reshape(B, NB, SWA_BLOCK, SWA_KV_HEADS, SWA_GROUP, SWA_HEAD_DIM)
    k = qkv[..., SWA_HEADS * SWA_HEAD_DIM:(SWA_HEADS + SWA_KV_HEADS) * SWA_HEAD_DIM]
    v = qkv[..., (SWA_HEADS + SWA_KV_HEADS) * SWA_HEAD_DIM:]
    k = k.reshape(B, NB, SWA_BLOCK, SWA_KV_HEADS, SWA_HEAD_DIM)
    v = v.reshape(B, NB, SWA_BLOCK, SWA_KV_HEADS, SWA_HEAD_DIM)
    pad = ((0, 0), (1, 0), (0, 0), (0, 0), (0, 0))
    kk = jnp.concatenate([jnp.pad(k[:, :-1], pad), k], axis=2)
    vv = jnp.concatenate([jnp.pad(v[:, :-1], pad), v], axis=2)
    logits = jnp.einsum('bnqhgd,bnkhd->bnhgqk', q, kk).astype(jnp.float32) * (SWA_HEAD_DIM ** -0.5)
    qi = jnp.arange(SWA_BLOCK)[:, None]
    kj = jnp.arange(2 * SWA_BLOCK)[None, :]
    dist = qi + SWA_BLOCK - kj
    bias = rel_bias[t5_causal_bucket(dist)].astype(jnp.float32)
    bias = bias.transpose(2, 0, 1).reshape(SWA_KV_HEADS, SWA_GROUP, SWA_BLOCK, 2 * SWA_BLOCK)
    key_pos = jnp.arange(NB)[:, None] * SWA_BLOCK - SWA_BLOCK + kj
    valid = ((dist >= 0) & (dist < WINDOW))[None] & (key_pos >= 0)[:, None, :]
    logits = jnp.where(valid[None, :, None, None], logits + bias, -jnp.inf)
    sink = sinks.astype(jnp.float32).reshape(SWA_KV_HEADS, SWA_GROUP)[None, None, :, :, None, None]
    m = jnp.maximum(jnp.max(logits, axis=-1, keepdims=True), sink)
    p = jnp.exp(logits - m)
    p = p / (jnp.sum(p, axis=-1, keepdims=True) + jnp.exp(sink - m))
    out = jnp.einsum('bnhgqk,bnkhd->bnqhgd', p.astype(vv.dtype), vv)
    out = out.reshape(B, S, SWA_HEADS * SWA_HEAD_DIM)
    return out @ w_out + b_out


def squared_relu_mlp(h, w_up, w_down):
    return jnp.square(jax.nn.relu(h @ w_up)) @ w_down


def setup_inputs(seed: int = 0) -> dict:
    key = jax.random.key(seed)
    ks = jax.random.split(key, 18)
    f32 = jnp.float32

    def nrm(k, shape, scale):
        return jax.random.normal(k, shape, f32) * scale

    x = nrm(ks[0], (BATCH, SEQ, D_MODEL), 1.0)
    norm_mix = 1.0 + nrm(ks[1], (DEPTH, D_MODEL), 0.02)
    norm_mlp = 1.0 + nrm(ks[2], (DEPTH, D_MODEL), 0.02)
    norm_final = 1.0 + nrm(ks[3], (D_MODEL,), 0.02)
    dn_w_in = nrm(ks[4], (N_DN_LAYERS, D_MODEL, DN_IN), D_MODEL ** -0.5)
    dn_conv_w = nrm(ks[5], (N_DN_LAYERS, DN_CONV, 3 * DN_WIDTH), DN_CONV ** -0.5)
    dn_a_log = jnp.log(jax.random.uniform(ks[6], (N_DN_LAYERS, DN_HEADS), f32, 1.0, 16.0))
    dt = jnp.exp(jax.random.uniform(ks[7], (N_DN_LAYERS, DN_HEADS), f32, math.log(1e-3), math.log(1e-1)))
    dn_dt_bias = dt + jnp.log(-jnp.expm1(-dt))
    dn_norm_w = 1.0 + nrm(ks[8], (N_DN_LAYERS, DN_HEAD_DIM), 0.02)
    dn_w_out = nrm(ks[9], (N_DN_LAYERS, DN_WIDTH, D_MODEL), DN_WIDTH ** -0.5)
    swa_w_qkv = nrm(ks[10], (N_SWA_LAYERS, D_MODEL, SWA_QKV), D_MODEL ** -0.5)
    swa_b_qkv = nrm(ks[11], (N_SWA_LAYERS, SWA_QKV), 0.02)
    swa_sinks = nrm(ks[12], (N_SWA_LAYERS, SWA_HEADS), 0.5)
    swa_w_out = nrm(ks[13], (N_SWA_LAYERS, SWA_HEADS * SWA_HEAD_DIM, D_MODEL), (SWA_HEADS * SWA_HEAD_DIM) ** -0.5)
    swa_b_out = nrm(ks[14], (N_SWA_LAYERS, D_MODEL), 0.02)
    rel_bias = nrm(ks[15], (NUM_BUCKETS, SWA_HEADS), 0.5)
    mlp_w_up = nrm(ks[16], (DEPTH, D_MODEL, D_FF), D_MODEL ** -0.5)
    mlp_w_down = nrm(ks[17], (DEPTH, D_FF, D_MODEL), D_FF ** -0.5)
    return {'x': x, 'norm_mix': norm_mix, 'norm_mlp': norm_mlp, 'norm_final': norm_final,
            'dn_w_in': dn_w_in, 'dn_conv_w': dn_conv_w, 'dn_a_log': dn_a_log, 'dn_dt_bias': dn_dt_bias,
            'dn_norm_w': dn_norm_w, 'dn_w_out': dn_w_out,
            'swa_w_qkv': swa_w_qkv, 'swa_b_qkv': swa_b_qkv, 'swa_sinks': swa_sinks,
            'swa_w_out': swa_w_out, 'swa_b_out': swa_b_out, 'rel_bias': rel_bias,
            'mlp_w_up': mlp_w_up, 'mlp_w_down': mlp_w_down}


def reference(x, norm_mix, norm_mlp, norm_final, dn_w_in, dn_conv_w, dn_a_log, dn_dt_bias, dn_norm_w,
              dn_w_out, swa_w_qkv, swa_b_qkv, swa_sinks, swa_w_out, swa_b_out, rel_bias,
              mlp_w_up, mlp_w_down):
    for i in range(DEPTH):
        h = rms_norm(x, norm_mix[i])
        j = i // N_MIXERS
        if i % N_MIXERS == 0:
            y = gated_deltanet(h, dn_w_in[j], dn_conv_w[j], dn_a_log[j], dn_dt_bias[j], dn_norm_w[j], dn_w_out[j])
        else:
            y = sliding_window_sink_attention(h, swa_w_qkv[j], swa_b_qkv[j], swa_sinks[j], swa_w_out[j],
                                              swa_b_out[j], rel_bias)
        x = x + y
        x = x + squared_relu_mlp(rms_norm(x, norm_mlp[i]), mlp_w_up[i], mlp_w_down[i])
    return rms_norm(x, norm_final)
```

```python
import functools
import math

import numpy as np
import jax
import jax.numpy as jnp
from jax import lax
from jax.experimental import pallas as pl
from jax.experimental.pallas import tpu as pltpu

F32 = jnp.float32
BF16 = jnp.bfloat16

D_MODEL = 1024
RMS_EPS = 1e-6
DN_HEAD_DIM = 128
DN_HEADS = 8
DN_WIDTH = DN_HEADS * DN_HEAD_DIM
DN_CONV = 4
DN_CHUNK = 64
SWA_HEAD_DIM = 64
SWA_HEADS = 16
SWA_KV_HEADS = 2
SWA_GROUP = SWA_HEADS // SWA_KV_HEADS
SWA_KV_WIDTH = SWA_KV_HEADS * SWA_HEAD_DIM
WINDOW = 128
NUM_BUCKETS = 32
MAX_DISTANCE = 128
D_FF = 4 * D_MODEL

V7X_VMEM_BYTES = 64 * 1024 * 1024
VMEM_LIMIT_BYTES = V7X_VMEM_BYTES * 3 // 4
SUBLANES = 8
LANES = 128

ROW_TILE = 512
FF_CHUNK = 1024
CONV_COLS = 512
DELTA_CHUNKS = 2


def _resident(shape):
    return pl.BlockSpec(shape, lambda *_: (0,) * len(shape), pipeline_mode=pl.Buffered(1))


def _params(*semantics):
    return pltpu.CompilerParams(dimension_semantics=semantics, vmem_limit_bytes=VMEM_LIMIT_BYTES)


def _rms_norm(x, w):
    return x * lax.rsqrt(jnp.mean(x * x, axis=-1, keepdims=True) + RMS_EPS) * w


def _sigmoid(x):
    return 1.0 / (1.0 + jnp.exp(-x))


def _dot(a, b):
    return jnp.dot(a, b, preferred_element_type=F32)


def _dot_bt(a, b):
    return lax.dot_general(a, b, (((1,), (1,)), ((), ())), preferred_element_type=F32)


def _dot_at(a, b):
    return lax.dot_general(a, b, (((0,), (0,)), ((), ())), preferred_element_type=F32)


def _out_mlp_kernel(x_ref, a_ref, wo_ref, bo_ref, nw_ref, wup_ref, wdn_ref, fw_ref, o_ref, *, final_norm):
    x1 = x_ref[...] + _dot(a_ref[...], wo_ref[...]) + bo_ref[...]
    h = _rms_norm(x1, nw_ref[...]).astype(BF16)
    acc = x1
    for c in range(D_FF // FF_CHUNK):
        cols = slice(c * FF_CHUNK, (c + 1) * FF_CHUNK)
        u = jnp.square(jnp.maximum(_dot(h, wup_ref[:, cols]), 0.0)).astype(BF16)
        acc = acc + _dot(u, wdn_ref[cols, :])
    if final_norm:
        acc = _rms_norm(acc, fw_ref[...])
    o_ref[...] = acc


def _out_mlp(x, a, w_out, b_out, norm_w, w_up, w_down, final_w, final_norm):
    m = x.shape[0]
    row = lambda i: (i, 0)
    return pl.pallas_call(
        functools.partial(_out_mlp_kernel, final_norm=final_norm),
        grid=(m // ROW_TILE,),
        in_specs=[
            pl.BlockSpec((ROW_TILE, D_MODEL), row),
            pl.BlockSpec((ROW_TILE, D_MODEL), row),
            _resident((D_MODEL, D_MODEL)),
            _resident((1, D_MODEL)),
            _resident((1, D_MODEL)),
            _resident((D_MODEL, D_FF)),
            _resident((D_FF, D_MODEL)),
            _resident((1, D_MODEL)),
        ],
        out_specs=pl.BlockSpec((ROW_TILE, D_MODEL), row),
        out_shape=jax.ShapeDtypeStruct((m, D_MODEL), F32),
        compiler_params=_params("parallel"),
        name="out_mlp",
    )(x, a, w_out, b_out, norm_w, w_up, w_down, final_w)


def _swa_qkv_kernel(x_ref, nw_ref, w_ref, b_ref, q_ref, kv_ref):
    h = _rms_norm(x_ref[...], nw_ref[...]).astype(BF16)
    y = _dot(h, w_ref[...]) + b_ref[...]
    q_ref[...] = (y[:, :D_MODEL] * (SWA_HEAD_DIM ** -0.5)).astype(BF16)
    kv_ref[...] = y[:, D_MODEL:].astype(BF16)


def _swa_qkv(x, norm_w, w_qkv, b_qkv):
    m = x.shape[0]
    n_out = D_MODEL + 2 * SWA_KV_WIDTH
    row = lambda i: (i, 0)
    return pl.pallas_call(
        _swa_qkv_kernel,
        grid=(m // ROW_TILE,),
        in_specs=[
            pl.BlockSpec((ROW_TILE, D_MODEL), row),
            _resident((1, D_MODEL)),
            _resident((D_MODEL, n_out)),
            _resident((1, n_out)),
        ],
        out_specs=[
            pl.BlockSpec((ROW_TILE, D_MODEL), row),
            pl.BlockSpec((ROW_TILE, 2 * SWA_KV_WIDTH), row),
        ],
        out_shape=[
            jax.ShapeDtypeStruct((m, D_MODEL), BF16),
            jax.ShapeDtypeStruct((m, 2 * SWA_KV_WIDTH), BF16),
        ],
        compiler_params=_params("parallel"),
        name="swa_qkv",
    )(x, norm_w, w_qkv, b_qkv)


def _t5_bucket_table():
    qi = np.arange(WINDOW)[:, None]
    kj = np.arange(2 * WINDOW)[None, :]
    dist = qi + WINDOW - kj
    n = np.maximum(dist, 0)
    max_exact = NUM_BUCKETS // 2
    ratio = np.log(np.maximum(n, 1).astype(np.float32) / np.float32(max_exact)) / np.float32(
        math.log(MAX_DISTANCE / max_exact))
    large = max_exact + (ratio * np.float32(NUM_BUCKETS - max_exact)).astype(np.int32)
    large = np.minimum(large, NUM_BUCKETS - 1)
    bucket = np.where(n < max_exact, n, large)
    valid = (dist >= 0) & (dist < WINDOW)
    return np.where(valid, bucket, -1).astype(np.int32)


def _t5_bias_kernel(bucket_ref, rel_ref, o_ref):
    head = pl.program_id(0)
    bucket = bucket_ref[...]
    acc = jnp.full(bucket.shape, -jnp.inf, F32)
    for b in range(NUM_BUCKETS):
        acc = jnp.where(bucket == b, rel_ref[b, head], acc)
    o_ref[0] = acc


def _t5_bias(rel_bias):
    shape = (WINDOW, 2 * WINDOW)
    return pl.pallas_call(
        _t5_bias_kernel,
        grid=(SWA_HEADS,),
        in_specs=[
            pl.BlockSpec(shape, lambda h: (0, 0)),
            pl.BlockSpec(memory_space=pltpu.SMEM),
        ],
        out_specs=pl.BlockSpec((1,) + shape, lambda h: (h, 0, 0)),
        out_shape=jax.ShapeDtypeStruct((SWA_HEADS,) + shape, F32),
        compiler_params=_params("parallel"),
        name="t5_bias",
    )(jnp.asarray(_t5_bucket_table()), rel_bias)


def _swa_attn_kernel(q_ref, kvc_ref, kvp_ref, bias_ref, sink_ref, o_ref):
    first_block = pl.program_id(1) == 0
    hd = SWA_HEAD_DIM
    keys = jnp.concatenate([kvp_ref[:, :SWA_KV_WIDTH], kvc_ref[:, :SWA_KV_WIDTH]], axis=0)
    vals = jnp.concatenate([kvp_ref[:, SWA_KV_WIDTH:], kvc_ref[:, SWA_KV_WIDTH:]], axis=0)
    col = lax.broadcasted_iota(jnp.int32, (WINDOW, 2 * WINDOW), 1)
    no_key = jnp.logical_and(first_block, col < WINDOW)
    zeros = jnp.zeros((2 * WINDOW, hd), BF16)

    for g in range(SWA_KV_HEADS):
        k_g = keys[:, g * hd:(g + 1) * hd]
        v_g = vals[:, g * hd:(g + 1) * hd]
        k_pad = (jnp.concatenate([k_g, zeros], axis=1), jnp.concatenate([zeros, k_g], axis=1))
        v_pad = (jnp.concatenate([v_g, zeros], axis=1), jnp.concatenate([zeros, v_g], axis=1))
        for j in range(SWA_GROUP // 2):
            pair = g * (SWA_GROUP // 2) + j
            lanes = slice(pair * 2 * hd, (pair + 1) * 2 * hd)
            q_pair = q_ref[:, lanes]
            out = None
            for e in range(2):
                head = 2 * pair + e
                s = _dot_bt(q_pair, k_pad[e]) + bias_ref[head]
                s = jnp.where(no_key, -jnp.inf, s)
                sink = sink_ref[head]
                m = jnp.maximum(jnp.max(s, axis=-1, keepdims=True), sink)
                p = jnp.exp(s - m)
                denom = jnp.sum(p, axis=-1, keepdims=True) + jnp.exp(sink - m)
                o = _dot(p.astype(BF16), v_pad[e]) / denom
                out = o if out is None else out + o
            o_ref[:, lanes] = out.astype(BF16)


def _swa_attn(q, kv, bias, sinks):
    b, s, _ = q.shape
    return pl.pallas_call(
        _swa_attn_kernel,
        grid=(b, s // WINDOW),
        in_specs=[
            pl.BlockSpec((None, WINDOW, D_MODEL), lambda i, n: (i, n, 0)),
            pl.BlockSpec((None, WINDOW, 2 * SWA_KV_WIDTH), lambda i, n: (i, n, 0)),
            pl.BlockSpec((None, WINDOW, 2 * SWA_KV_WIDTH), lambda i, n: (i, jnp.maximum(n - 1, 0), 0)),
            _resident((SWA_HEADS, WINDOW, 2 * WINDOW)),
            pl.BlockSpec(memory_space=pltpu.SMEM),
        ],
        out_specs=pl.BlockSpec((None, WINDOW, D_MODEL), lambda i, n: (i, n, 0)),
        out_shape=jax.ShapeDtypeStruct((b, s, D_MODEL), BF16),
        compiler_params=_params("parallel", "parallel"),
        name="swa_attn",
    )(q, kv, kv, bias, sinks)


def _dn_in_kernel(x_ref, nw_ref, w_ref, wba_ref, cw_ref, alog_ref, dt_ref, qkv_ref, z_ref, bg_ref,
                  halo_ref, pbuf_ref):
    tm = ROW_TILE

    @pl.when(pl.program_id(1) == 0)
    def _():
        halo_ref[...] = jnp.zeros_like(halo_ref)

    h = _rms_norm(x_ref[...], nw_ref[...]).astype(BF16)

    for c in range(3 * DN_WIDTH // CONV_COLS):
        c0 = c * CONV_COLS
        cols = slice(c0, c0 + CONV_COLS)
        p = _dot(h, w_ref[:, cols])
        pbuf_ref[0:SUBLANES, :] = halo_ref[:, cols]
        pbuf_ref[SUBLANES:SUBLANES + tm, :] = p
        halo_ref[:, cols] = p[tm - SUBLANES:, :]
        y = cw_ref[DN_CONV - 1:DN_CONV, cols] * p
        for j in range(DN_CONV - 1):
            start = SUBLANES - (DN_CONV - 1) + j
            y = y + cw_ref[j:j + 1, cols] * pbuf_ref[start:start + tm, :]
        y = y * _sigmoid(y)
        if c0 < 2 * DN_WIDTH:
            scale = DN_HEAD_DIM ** -0.5 if c0 < DN_WIDTH else 1.0
            for hh in range(CONV_COLS // DN_HEAD_DIM):
                t = y[:, hh * DN_HEAD_DIM:(hh + 1) * DN_HEAD_DIM]
                r = lax.rsqrt(jnp.sum(t * t, axis=-1, keepdims=True) + RMS_EPS) * scale
                qkv_ref[:, c0 + hh * DN_HEAD_DIM:c0 + (hh + 1) * DN_HEAD_DIM] = (t * r).astype(BF16)
        else:
            qkv_ref[:, cols] = y.astype(BF16)

    z_ref[...] = _dot(h, w_ref[:, 3 * DN_WIDTH:]).astype(BF16)

    pb = _dot(h, wba_ref[...])
    beta = _sigmoid(pb)
    t = pb + dt_ref[...]
    softplus = jnp.maximum(t, 0.0) + jnp.log(1.0 + jnp.exp(-jnp.abs(t)))
    g = -jnp.exp(alog_ref[...]) * softplus
    ri = lax.broadcasted_iota(jnp.int32, (DN_CHUNK, DN_CHUNK), 0)
    ci = lax.broadcasted_iota(jnp.int32, (DN_CHUNK, DN_CHUNK), 1)
    tri = (ri >= ci).astype(F32)
    lane = lax.broadcasted_iota(jnp.int32, (DN_CHUNK, LANES), 1)
    for c in range(tm // DN_CHUNK):
        rows = slice(c * DN_CHUNK, (c + 1) * DN_CHUNK)
        gc = jnp.dot(tri, g[rows], preferred_element_type=F32, precision=lax.Precision.HIGHEST)
        bg_ref[rows, :] = jnp.where(lane < DN_HEADS, beta[rows], gc)[:, :2 * DN_HEADS]


def _dn_in(x, norm_w, w_qkvz, w_ba, conv_w, alog_row, dt_row):
    b, s, _ = x.shape
    tok = lambda i, n: (i, n, 0)
    return pl.pallas_call(
        _dn_in_kernel,
        grid=(b, s // ROW_TILE),
        in_specs=[
            pl.BlockSpec((None, ROW_TILE, D_MODEL), tok),
            _resident((1, D_MODEL)),
            _resident((D_MODEL, 4 * DN_WIDTH)),
            _resident((D_MODEL, LANES)),
            _resident((DN_CONV, 3 * DN_WIDTH)),
            _resident((1, LANES)),
            _resident((1, LANES)),
        ],
        out_specs=[
            pl.BlockSpec((None, ROW_TILE, 3 * DN_WIDTH), tok),
            pl.BlockSpec((None, ROW_TILE, DN_WIDTH), tok),
            pl.BlockSpec((None, ROW_TILE, 2 * DN_HEADS), tok),
        ],
        out_shape=[
            jax.ShapeDtypeStruct((b, s, 3 * DN_WIDTH), BF16),
            jax.ShapeDtypeStruct((b, s, DN_WIDTH), BF16),
            jax.ShapeDtypeStruct((b, s, 2 * DN_HEADS), F32),
        ],
        scratch_shapes=[
            pltpu.VMEM((SUBLANES, 3 * DN_WIDTH), F32),
            pltpu.VMEM((SUBLANES + ROW_TILE, CONV_COLS), F32),
        ],
        compiler_params=_params("arbitrary", "arbitrary"),
        name="dn_in",
    )(x, norm_w, w_qkvz, w_ba, conv_w, alog_row, dt_row)


def _delta_kernel(qkv_ref, z_ref, bg_ref, bgt_ref, nw_ref, o_ref, state_ref):
    c_len, d = DN_CHUNK, DN_HEAD_DIM

    @pl.when(pl.program_id(1) == 0)
    def _():
        state_ref[...] = jnp.zeros_like(state_ref)

    ri = lax.broadcasted_iota(jnp.int32, (c_len, c_len), 0)
    ci = lax.broadcasted_iota(jnp.int32, (c_len, c_len), 1)
    causal = ri >= ci
    strict = ri > ci
    eye = (ri == ci).astype(F32)
    sub_blocks = [
        ((ri >> (k + 1)) == (ci >> (k + 1))) & (((ri >> k) & 1) == 1) & (((ci >> k) & 1) == 0)
        for k in range(int(math.log2(c_len)))
    ]
    norm_w = nw_ref[...]

    for h in range(DN_HEADS):
        lanes = slice(h * d, (h + 1) * d)
        staged = []
        for c in range(DELTA_CHUNKS):
            rows = slice(c * c_len, (c + 1) * c_len)
            q = qkv_ref[rows, h * d:(h + 1) * d]
            k = qkv_ref[rows, DN_WIDTH + h * d:DN_WIDTH + (h + 1) * d]
            v = qkv_ref[rows, 2 * DN_WIDTH + h * d:2 * DN_WIDTH + (h + 1) * d]
            beta = bg_ref[rows, h:h + 1]
            gc_col = bg_ref[rows, DN_HEADS + h:DN_HEADS + h + 1]
            gc_row = bgt_ref[DN_HEADS + h:DN_HEADS + h + 1, rows]
            g_last = gc_row[:, c_len - 1:]

            kf = k.astype(F32)
            k_beta = kf * beta
            decay = jnp.exp(jnp.where(causal, gc_col - gc_row, -jnp.inf))
            qk = _dot_bt(jnp.concatenate([q, k_beta.astype(BF16)], axis=0), k)
            attn = qk[:c_len] * decay
            neg_lower = jnp.where(strict, -(qk[c_len:] * decay), 0.0)

            inv = eye + jnp.where(sub_blocks[0], neg_lower, 0.0)
            for mask in sub_blocks[1:]:
                inv_b = inv.astype(BF16)
                n_t = _dot(jnp.where(mask, neg_lower, 0.0).astype(BF16), inv_b)
                inv = inv + _dot(inv_b, n_t.astype(BF16))

            grow = jnp.exp(gc_col)
            rhs = jnp.concatenate([v.astype(F32) * beta, k_beta * grow], axis=1).astype(BF16)
            sol = _dot(inv.astype(BF16), rhs)
            u, w = sol[:, :d], sol[:, d:]
            q_dec = q.astype(F32) * grow
            k_dec = kf * jnp.exp(g_last - gc_col)
            staged.append((rows, u, w, attn, q_dec, k_dec, g_last))

        state = state_ref[h]
        for rows, u, w, attn, q_dec, k_dec, g_last in staged:
            ws_qs = _dot(jnp.concatenate([w, q_dec], axis=0).astype(BF16), state.astype(BF16))
            v_new = (u - ws_qs[:c_len]).astype(BF16)
            o = ws_qs[c_len:] + _dot(attn.astype(BF16), v_new)
            state = state * jnp.exp(g_last) + _dot_at(k_dec.astype(BF16), v_new)
            zf = z_ref[rows, lanes].astype(F32)
            o = o * lax.rsqrt(jnp.mean(o * o, axis=-1, keepdims=True) + RMS_EPS) * norm_w * (zf * _sigmoid(zf))
            o_ref[rows, lanes] = o.astype(BF16)
        state_ref[h] = state


def _delta_rule(qkv, z, bg, norm_w):
    b, s, _ = qkv.shape
    t = DELTA_CHUNKS * DN_CHUNK
    ns = s // t
    bg4 = bg.reshape(b, ns, t, 2 * DN_HEADS)
    bgt4 = jnp.swapaxes(bg4, 2, 3)
    tok = lambda i, n: (i, n, 0)
    tok4 = lambda i, n: (i, n, 0, 0)
    return pl.pallas_call(
        _delta_kernel,
        grid=(b, ns),
        in_specs=[
            pl.BlockSpec((None, t, 3 * DN_WIDTH), tok),
            pl.BlockSpec((None, t, DN_WIDTH), tok),
            pl.BlockSpec((None, None, t, 2 * DN_HEADS), tok4),
            pl.BlockSpec((None, None, 2 * DN_HEADS, t), tok4),
            _resident((1, DN_HEAD_DIM)),
        ],
        out_specs=pl.BlockSpec((None, t, DN_WIDTH), tok),
        out_shape=jax.ShapeDtypeStruct((b, s, DN_WIDTH), BF16),
        scratch_shapes=[pltpu.VMEM((DN_HEADS, DN_HEAD_DIM, DN_HEAD_DIM), F32)],
        compiler_params=_params("arbitrary", "arbitrary"),
        name="delta_rule",
    )(qkv, z, bg4, bgt4, norm_w)


def _row(v):
    return v.reshape(1, -1).astype(F32)


def _gate_row(v):
    return jnp.zeros((1, LANES), F32).at[0, DN_HEADS:2 * DN_HEADS].set(v.astype(F32))


def kernel(x, norm_mix, norm_mlp, norm_final, dn_w_in, dn_conv_w, dn_a_log, dn_dt_bias, dn_norm_w, dn_w_out,
           swa_w_qkv, swa_b_qkv, swa_sinks, swa_w_out, swa_b_out, rel_bias, mlp_w_up, mlp_w_down):
    b, s, d = x.shape
    depth = norm_mix.shape[0]
    m = b * s
    bias = _t5_bias(rel_bias.astype(F32))
    zero_bias = jnp.zeros((1, d), F32)
    final_w = _row(norm_final)
    for i in range(depth):
        j = i // 2
        if i % 2 == 0:
            w_in = dn_w_in[j]
            w_ba = jnp.zeros((d, LANES), BF16).at[:, :2 * DN_HEADS].set(w_in[:, 4 * DN_WIDTH:].astype(BF16))
            qkv, z, bg = _dn_in(x, _row(norm_mix[i]), w_in[:, :4 * DN_WIDTH].astype(BF16), w_ba,
                                dn_conv_w[j].astype(F32), _gate_row(dn_a_log[j]), _gate_row(dn_dt_bias[j]))
            mixed = _delta_rule(qkv, z, bg, _row(dn_norm_w[j]))
            w_out, b_out = dn_w_out[j], zero_bias
        else:
            q, kv = _swa_qkv(x.reshape(m, d), _row(norm_mix[i]), swa_w_qkv[j].astype(BF16), _row(swa_b_qkv[j]))
            mixed = _swa_attn(q.reshape(b, s, d), kv.reshape(b, s, 2 * SWA_KV_WIDTH), bias,
                              swa_sinks[j].astype(F32))
            w_out, b_out = swa_w_out[j], _row(swa_b_out[j])
        x = _out_mlp(x.reshape(m, d), mixed.reshape(m, d), w_out.astype(BF16), b_out, _row(norm_mlp[i]),
                     mlp_w_up[i].astype(BF16), mlp_w_down[i].astype(BF16), final_w,
                     final_norm=(i == depth - 1)).reshape(b, s, d)
    return x
```

```python
import functools
import math

import numpy as np
import jax
import jax.numpy as jnp
from jax import lax
from jax.experimental import pallas as pl
from jax.experimental.pallas import tpu as pltpu

F32 = jnp.float32
BF16 = jnp.bfloat16

D_MODEL = 1024
RMS_EPS = 1e-6
DN_HEAD_DIM = 128
DN_HEADS = 8
DN_WIDTH = DN_HEADS * DN_HEAD_DIM
DN_CONV = 4
SWA_HEAD_DIM = 64
SWA_HEADS = 16
SWA_KV_HEADS = 2
SWA_GROUP = SWA_HEADS // SWA_KV_HEADS
SWA_KV_WIDTH = SWA_KV_HEADS * SWA_HEAD_DIM
WINDOW = 128
NUM_BUCKETS = 32
MAX_DISTANCE = 128
D_FF = 4 * D_MODEL

V7X_VMEM_BYTES = 64 * 1024 * 1024
VMEM_LIMIT_BYTES = V7X_VMEM_BYTES * 3 // 4
SUBLANES = 8
LANES = 128

ROW_TILE = 512
FF_CHUNK = 1024
CONV_COLS = 512
DELTA_CHUNK = 128
DELTA_CHUNKS = 2


def _resident(shape):
    return pl.BlockSpec(shape, lambda *_: (0,) * len(shape), pipeline_mode=pl.Buffered(1))


def _params(*semantics):
    return pltpu.CompilerParams(dimension_semantics=semantics, vmem_limit_bytes=VMEM_LIMIT_BYTES)


def _rms_norm(x, w):
    return x * lax.rsqrt(jnp.mean(x * x, axis=-1, keepdims=True) + RMS_EPS) * w


def _sigmoid(x):
    return 1.0 / (1.0 + jnp.exp(-x))


def _dot(a, b):
    return jnp.dot(a, b, preferred_element_type=F32)


def _dot_bt(a, b):
    return lax.dot_general(a, b, (((1,), (1,)), ((), ())), preferred_element_type=F32)


def _dot_at(a, b):
    return lax.dot_general(a, b, (((0,), (0,)), ((), ())), preferred_element_type=F32)


def _out_mlp_kernel(x_ref, a_ref, wo_ref, bo_ref, nw_ref, wup_ref, wdn_ref, fw_ref, o_ref, *, final_norm):
    x1 = x_ref[...] + _dot(a_ref[...], wo_ref[...]) + bo_ref[...]
    h = _rms_norm(x1, nw_ref[...]).astype(BF16)
    acc = x1
    for c in range(D_FF // FF_CHUNK):
        cols = slice(c * FF_CHUNK, (c + 1) * FF_CHUNK)
        u = jnp.square(jnp.maximum(_dot(h, wup_ref[:, cols]), 0.0)).astype(BF16)
        acc = acc + _dot(u, wdn_ref[cols, :])
    if final_norm:
        acc = _rms_norm(acc, fw_ref[...])
    o_ref[...] = acc


def _out_mlp(x, a, w_out, b_out, norm_w, w_up, w_down, final_w, final_norm):
    m = x.shape[0]
    row = lambda i: (i, 0)
    return pl.pallas_call(
        functools.partial(_out_mlp_kernel, final_norm=final_norm),
        grid=(m // ROW_TILE,),
        in_specs=[
            pl.BlockSpec((ROW_TILE, D_MODEL), row),
            pl.BlockSpec((ROW_TILE, D_MODEL), row),
            _resident((D_MODEL, D_MODEL)),
            _resident((1, D_MODEL)),
            _resident((1, D_MODEL)),
            _resident((D_MODEL, D_FF)),
            _resident((D_FF, D_MODEL)),
            _resident((1, D_MODEL)),
        ],
        out_specs=pl.BlockSpec((ROW_TILE, D_MODEL), row),
        out_shape=jax.ShapeDtypeStruct((m, D_MODEL), F32),
        compiler_params=_params("parallel"),
        name="out_mlp",
    )(x, a, w_out, b_out, norm_w, w_up, w_down, final_w)


def _swa_qkv_kernel(x_ref, nw_ref, w_ref, b_ref, q_ref, kv_ref):
    h = _rms_norm(x_ref[...], nw_ref[...]).astype(BF16)
    y = _dot(h, w_ref[...]) + b_ref[...]
    q_ref[...] = (y[:, :D_MODEL] * (SWA_HEAD_DIM ** -0.5)).astype(BF16)
    kv_ref[...] = y[:, D_MODEL:].astype(BF16)


def _swa_qkv(x, norm_w, w_qkv, b_qkv):
    m = x.shape[0]
    n_out = D_MODEL + 2 * SWA_KV_WIDTH
    row = lambda i: (i, 0)
    return pl.pallas_call(
        _swa_qkv_kernel,
        grid=(m // ROW_TILE,),
        in_specs=[
            pl.BlockSpec((ROW_TILE, D_MODEL), row),
            _resident((1, D_MODEL)),
            _resident((D_MODEL, n_out)),
            _resident((1, n_out)),
        ],
        out_specs=[
            pl.BlockSpec((ROW_TILE, D_MODEL), row),
            pl.BlockSpec((ROW_TILE, 2 * SWA_KV_WIDTH), row),
        ],
        out_shape=[
            jax.ShapeDtypeStruct((m, D_MODEL), BF16),
            jax.ShapeDtypeStruct((m, 2 * SWA_KV_WIDTH), BF16),
        ],
        compiler_params=_params("parallel"),
        name="swa_qkv",
    )(x, norm_w, w_qkv, b_qkv)


def _t5_bucket_table():
    qi = np.arange(WINDOW)[:, None]
    kj = np.arange(2 * WINDOW)[None, :]
    dist = qi + WINDOW - kj
    n = np.maximum(dist, 0)
    max_exact = NUM_BUCKETS // 2
    ratio = np.log(np.maximum(n, 1).astype(np.float32) / np.float32(max_exact)) / np.float32(
        math.log(MAX_DISTANCE / max_exact))
    large = max_exact + (ratio * np.float32(NUM_BUCKETS - max_exact)).astype(np.int32)
    large = np.minimum(large, NUM_BUCKETS - 1)
    bucket = np.where(n < max_exact, n, large)
    valid = (dist >= 0) & (dist < WINDOW)
    return np.where(valid, bucket, -1).astype(np.int32)


def _t5_bias_kernel(bucket_ref, rel_ref, o_ref):
    head = pl.program_id(0)
    bucket = bucket_ref[...]
    acc = jnp.full(bucket.shape, -jnp.inf, F32)
    for b in range(NUM_BUCKETS):
        acc = jnp.where(bucket == b, rel_ref[b, head], acc)
    o_ref[0] = acc


def _t5_bias(rel_bias):
    shape = (WINDOW, 2 * WINDOW)
    return pl.pallas_call(
        _t5_bias_kernel,
        grid=(SWA_HEADS,),
        in_specs=[
            pl.BlockSpec(shape, lambda h: (0, 0)),
            pl.BlockSpec(memory_space=pltpu.SMEM),
        ],
        out_specs=pl.BlockSpec((1,) + shape, lambda h: (h, 0, 0)),
        out_shape=jax.ShapeDtypeStruct((SWA_HEADS,) + shape, F32),
        compiler_params=_params("parallel"),
        name="t5_bias",
    )(jnp.asarray(_t5_bucket_table()), rel_bias)


def _swa_attn_kernel(q_ref, kvc_ref, kvp_ref, bias_ref, sink_ref, o_ref):
    first_block = pl.program_id(1) == 0
    hd = SWA_HEAD_DIM
    keys = jnp.concatenate([kvp_ref[:, :SWA_KV_WIDTH], kvc_ref[:, :SWA_KV_WIDTH]], axis=0)
    vals = jnp.concatenate([kvp_ref[:, SWA_KV_WIDTH:], kvc_ref[:, SWA_KV_WIDTH:]], axis=0)
    col = lax.broadcasted_iota(jnp.int32, (WINDOW, 2 * WINDOW), 1)
    no_key = jnp.logical_and(first_block, col < WINDOW)
    zeros = jnp.zeros((2 * WINDOW, hd), BF16)

    for g in range(SWA_KV_HEADS):
        k_g = keys[:, g * hd:(g + 1) * hd]
        v_g = vals[:, g * hd:(g + 1) * hd]
        k_pad = (jnp.concatenate([k_g, zeros], axis=1), jnp.concatenate([zeros, k_g], axis=1))
        v_pad = (jnp.concatenate([v_g, zeros], axis=1), jnp.concatenate([zeros, v_g], axis=1))
        for j in range(SWA_GROUP // 2):
            pair = g * (SWA_GROUP // 2) + j
            lanes = slice(pair * 2 * hd, (pair + 1) * 2 * hd)
            q_pair = q_ref[:, lanes]
            out = None
            for e in range(2):
                head = 2 * pair + e
                s = _dot_bt(q_pair, k_pad[e]) + bias_ref[head]
                s = jnp.where(no_key, -jnp.inf, s)
                sink = sink_ref[head]
                m = jnp.maximum(jnp.max(s, axis=-1, keepdims=True), sink)
                p = jnp.exp(s - m)
                denom = jnp.sum(p, axis=-1, keepdims=True) + jnp.exp(sink - m)
                o = _dot(p.astype(BF16), v_pad[e]) / denom
                out = o if out is None else out + o
            o_ref[:, lanes] = out.astype(BF16)


def _swa_attn(q, kv, bias, sinks):
    b, s, _ = q.shape
    return pl.pallas_call(
        _swa_attn_kernel,
        grid=(b, s // WINDOW),
        in_specs=[
            pl.BlockSpec((None, WINDOW, D_MODEL), lambda i, n: (i, n, 0)),
            pl.BlockSpec((None, WINDOW, 2 * SWA_KV_WIDTH), lambda i, n: (i, n, 0)),
            pl.BlockSpec((None, WINDOW, 2 * SWA_KV_WIDTH), lambda i, n: (i, jnp.maximum(n - 1, 0), 0)),
            _resident((SWA_HEADS, WINDOW, 2 * WINDOW)),
            pl.BlockSpec(memory_space=pltpu.SMEM),
        ],
        out_specs=pl.BlockSpec((None, WINDOW, D_MODEL), lambda i, n: (i, n, 0)),
        out_shape=jax.ShapeDtypeStruct((b, s, D_MODEL), BF16),
        compiler_params=_params("parallel", "parallel"),
        name="swa_attn",
    )(q, kv, kv, bias, sinks)


def _dn_in_kernel(x_ref, nw_ref, w_ref, wba_ref, cw_ref, alog_ref, dt_ref, qkv_ref, z_ref, bg_ref,
                  halo_ref, pbuf_ref):
    tm = ROW_TILE

    @pl.when(pl.program_id(1) == 0)
    def _():
        halo_ref[...] = jnp.zeros_like(halo_ref)

    h = _rms_norm(x_ref[...], nw_ref[...]).astype(BF16)

    for c in range(3 * DN_WIDTH // CONV_COLS):
        c0 = c * CONV_COLS
        cols = slice(c0, c0 + CONV_COLS)
        p = _dot(h, w_ref[:, cols])
        pbuf_ref[0:SUBLANES, :] = halo_ref[:, cols]
        pbuf_ref[SUBLANES:SUBLANES + tm, :] = p
        halo_ref[:, cols] = p[tm - SUBLANES:, :]
        y = cw_ref[DN_CONV - 1:DN_CONV, cols] * p
        for j in range(DN_CONV - 1):
            start = SUBLANES - (DN_CONV - 1) + j
            y = y + cw_ref[j:j + 1, cols] * pbuf_ref[start:start + tm, :]
        y = y * _sigmoid(y)
        if c0 < 2 * DN_WIDTH:
            scale = DN_HEAD_DIM ** -0.5 if c0 < DN_WIDTH else 1.0
            for hh in range(CONV_COLS // DN_HEAD_DIM):
                t = y[:, hh * DN_HEAD_DIM:(hh + 1) * DN_HEAD_DIM]
                r = lax.rsqrt(jnp.sum(t * t, axis=-1, keepdims=True) + RMS_EPS) * scale
                qkv_ref[:, c0 + hh * DN_HEAD_DIM:c0 + (hh + 1) * DN_HEAD_DIM] = (t * r).astype(BF16)
        else:
            qkv_ref[:, cols] = y.astype(BF16)

    z_ref[...] = _dot(h, w_ref[:, 3 * DN_WIDTH:]).astype(BF16)

    pb = _dot(h, wba_ref[...])
    beta = _sigmoid(pb)
    t = pb + dt_ref[...]
    softplus = jnp.maximum(t, 0.0) + jnp.log(1.0 + jnp.exp(-jnp.abs(t)))
    g = -jnp.exp(alog_ref[...]) * softplus
    ri = lax.broadcasted_iota(jnp.int32, (DELTA_CHUNK, DELTA_CHUNK), 0)
    ci = lax.broadcasted_iota(jnp.int32, (DELTA_CHUNK, DELTA_CHUNK), 1)
    tri = (ri >= ci).astype(F32)
    lane = lax.broadcasted_iota(jnp.int32, (DELTA_CHUNK, LANES), 1)
    for c in range(tm // DELTA_CHUNK):
        rows = slice(c * DELTA_CHUNK, (c + 1) * DELTA_CHUNK)
        gc = jnp.dot(tri, g[rows], preferred_element_type=F32, precision=lax.Precision.HIGHEST)
        bg_ref[rows, :] = jnp.where(lane < DN_HEADS, beta[rows], gc)[:, :2 * DN_HEADS]


def _dn_in(x, norm_w, w_qkvz, w_ba, conv_w, alog_row, dt_row):
    b, s, _ = x.shape
    tok = lambda i, n: (i, n, 0)
    return pl.pallas_call(
        _dn_in_kernel,
        grid=(b, s // ROW_TILE),
        in_specs=[
            pl.BlockSpec((None, ROW_TILE, D_MODEL), tok),
            _resident((1, D_MODEL)),
            _resident((D_MODEL, 4 * DN_WIDTH)),
            _resident((D_MODEL, LANES)),
            _resident((DN_CONV, 3 * DN_WIDTH)),
            _resident((1, LANES)),
            _resident((1, LANES)),
        ],
        out_specs=[
            pl.BlockSpec((None, ROW_TILE, 3 * DN_WIDTH), tok),
            pl.BlockSpec((None, ROW_TILE, DN_WIDTH), tok),
            pl.BlockSpec((None, ROW_TILE, 2 * DN_HEADS), tok),
        ],
        out_shape=[
            jax.ShapeDtypeStruct((b, s, 3 * DN_WIDTH), BF16),
            jax.ShapeDtypeStruct((b, s, DN_WIDTH), BF16),
            jax.ShapeDtypeStruct((b, s, 2 * DN_HEADS), F32),
        ],
        scratch_shapes=[
            pltpu.VMEM((SUBLANES, 3 * DN_WIDTH), F32),
            pltpu.VMEM((SUBLANES + ROW_TILE, CONV_COLS), F32),
        ],
        compiler_params=_params("arbitrary", "arbitrary"),
        name="dn_in",
    )(x, norm_w, w_qkvz, w_ba, conv_w, alog_row, dt_row)


def _delta_kernel(qkv_ref, z_ref, bg_ref, bgt_ref, nw_ref, o_ref, state_ref):
    c_len, d = DELTA_CHUNK, DN_HEAD_DIM

    @pl.when(pl.program_id(1) == 0)
    def _():
        state_ref[...] = jnp.zeros_like(state_ref)

    ri = lax.broadcasted_iota(jnp.int32, (c_len, c_len), 0)
    ci = lax.broadcasted_iota(jnp.int32, (c_len, c_len), 1)
    causal = ri >= ci
    strict = ri > ci
    eye = (ri == ci).astype(F32)
    sub_blocks = [
        ((ri >> (k + 1)) == (ci >> (k + 1))) & (((ri >> k) & 1) == 1) & (((ci >> k) & 1) == 0)
        for k in range(int(math.log2(c_len)))
    ]
    norm_w = nw_ref[...]

    def operands(h, c):
        rows = slice(c * c_len, (c + 1) * c_len)
        q = qkv_ref[rows, h * d:(h + 1) * d]
        k = qkv_ref[rows, DN_WIDTH + h * d:DN_WIDTH + (h + 1) * d]
        v = qkv_ref[rows, 2 * DN_WIDTH + h * d:2 * DN_WIDTH + (h + 1) * d]
        beta = bg_ref[rows, h:h + 1]
        gc_col = bg_ref[rows, DN_HEADS + h:DN_HEADS + h + 1]
        gc_row = bgt_ref[DN_HEADS + h:DN_HEADS + h + 1, rows]
        return rows, q, k, v, beta, gc_col, gc_row

    problems = [(h, c) for c in range(DELTA_CHUNKS) for h in range(DN_HEADS)]

    attn, neg_lower = {}, {}
    for p in problems:
        _, q, k, _, beta, gc_col, gc_row = operands(*p)
        k_beta = (k.astype(F32) * beta).astype(BF16)
        decay = jnp.exp(jnp.where(causal, gc_col - gc_row, -jnp.inf))
        qk = _dot_bt(jnp.concatenate([q, k_beta], axis=0), k)
        attn[p] = (qk[:c_len] * decay).astype(BF16)
        neg_lower[p] = jnp.where(strict, -(qk[c_len:] * decay), 0.0)

    inv = {p: eye + jnp.where(sub_blocks[0], neg_lower[p], 0.0) for p in problems}
    for mask in sub_blocks[1:]:
        n_t = {p: _dot(jnp.where(mask, neg_lower[p], 0.0).astype(BF16), inv[p].astype(BF16)) for p in problems}
        inv = {p: inv[p] + _dot(inv[p].astype(BF16), n_t[p].astype(BF16)) for p in problems}

    sol = {}
    for p in problems:
        _, _, k, v, beta, gc_col, _ = operands(*p)
        k_beta = k.astype(F32) * beta
        rhs = jnp.concatenate([v.astype(F32) * beta, k_beta * jnp.exp(gc_col)], axis=1).astype(BF16)
        sol[p] = _dot(inv[p].astype(BF16), rhs)

    for p in problems:
        h = p[0]
        lanes = slice(h * d, (h + 1) * d)
        rows, q, k, _, _, gc_col, gc_row = operands(*p)
        g_last = gc_row[:, c_len - 1:]
        u, w = sol[p][:, :d], sol[p][:, d:]
        q_dec = q.astype(F32) * jnp.exp(gc_col)
        k_dec = (k.astype(F32) * jnp.exp(g_last - gc_col)).astype(BF16)
        state = state_ref[h]
        ws_qs = _dot(jnp.concatenate([w, q_dec], axis=0).astype(BF16), state.astype(BF16))
        v_new = (u - ws_qs[:c_len]).astype(BF16)
        o = ws_qs[c_len:] + _dot(attn[p], v_new)
        state_ref[h] = state * jnp.exp(g_last) + _dot_at(k_dec, v_new)
        zf = z_ref[rows, lanes].astype(F32)
        o = o * lax.rsqrt(jnp.mean(o * o, axis=-1, keepdims=True) + RMS_EPS) * norm_w * (zf * _sigmoid(zf))
        o_ref[rows, lanes] = o.astype(BF16)


def _delta_rule(qkv, z, bg, norm_w):
    b, s, _ = qkv.shape
    t = DELTA_CHUNKS * DELTA_CHUNK
    ns = s // t
    bg4 = bg.reshape(b, ns, t, 2 * DN_HEADS)
    bgt4 = jnp.swapaxes(bg4, 2, 3)
    tok = lambda i, n: (i, n, 0)
    tok4 = lambda i, n: (i, n, 0, 0)
    return pl.pallas_call(
        _delta_kernel,
        grid=(b, ns),
        in_specs=[
            pl.BlockSpec((None, t, 3 * DN_WIDTH), tok),
            pl.BlockSpec((None, t, DN_WIDTH), tok),
            pl.BlockSpec((None, None, t, 2 * DN_HEADS), tok4),
            pl.BlockSpec((None, None, 2 * DN_HEADS, t), tok4),
            _resident((1, DN_HEAD_DIM)),
        ],
        out_specs=pl.BlockSpec((None, t, DN_WIDTH), tok),
        out_shape=jax.ShapeDtypeStruct((b, s, DN_WIDTH), BF16),
        scratch_shapes=[pltpu.VMEM((DN_HEADS, DN_HEAD_DIM, DN_HEAD_DIM), F32)],
        compiler_params=_params("arbitrary", "arbitrary"),
        name="delta_rule",
    )(qkv, z, bg4, bgt4, norm_w)


def _row(v):
    return v.reshape(1, -1).astype(F32)


def _gate_row(v):
    return jnp.zeros((1, LANES), F32).at[0, DN_HEADS:2 * DN_HEADS].set(v.astype(F32))


def kernel(x, norm_mix, norm_mlp, norm_final, dn_w_in, dn_conv_w, dn_a_log, dn_dt_bias, dn_norm_w, dn_w_out,
           swa_w_qkv, swa_b_qkv, swa_sinks, swa_w_out, swa_b_out, rel_bias, mlp_w_up, mlp_w_down):
    b, s, d = x.shape
    depth = norm_mix.shape[0]
    m = b * s
    bias = _t5_bias(rel_bias.astype(F32))
    zero_bias = jnp.zeros((1, d), F32)
    final_w = _row(norm_final)
    for i in range(depth):
        j = i // 2
        if i % 2 == 0:
            w_in = dn_w_in[j]
            w_ba = jnp.zeros((d, LANES), BF16).at[:, :2 * DN_HEADS].set(w_in[:, 4 * DN_WIDTH:].astype(BF16))
            qkv, z, bg = _dn_in(x, _row(norm_mix[i]), w_in[:, :4 * DN_WIDTH].astype(BF16), w_ba,
                                dn_conv_w[j].astype(F32), _gate_row(dn_a_log[j]), _gate_row(dn_dt_bias[j]))
            mixed = _delta_rule(qkv, z, bg, _row(dn_norm_w[j]))
            w_out, b_out = dn_w_out[j], zero_bias
        else:
            q, kv = _swa_qkv(x.reshape(m, d), _row(norm_mix[i]), swa_w_qkv[j].astype(BF16), _row(swa_b_qkv[j]))
            mixed = _swa_attn(q.reshape(b, s, d), kv.reshape(b, s, 2 * SWA_KV_WIDTH), bias,
                              swa_sinks[j].astype(F32))
            w_out, b_out = swa_w_out[j], _row(swa_b_out[j])
        x = _out_mlp(x.reshape(m, d), mixed.reshape(m, d), w_out.astype(BF16), b_out, _row(norm_mlp[i]),
                     mlp_w_up[i].astype(BF16), mlp_w_down[i].astype(BF16), final_w,
                     final_norm=(i == depth - 1)).reshape(b, s, d)
    return x
```

```python
import functools
import math

import numpy as np
import jax
import jax.numpy as jnp
from jax import lax
from jax.experimental import pallas as pl
from jax.experimental.pallas import tpu as pltpu

F32 = jnp.float32
BF16 = jnp.bfloat16

D_MODEL = 1024
RMS_EPS = 1e-6
DN_HEAD_DIM = 128
DN_HEADS = 8
DN_WIDTH = DN_HEADS * DN_HEAD_DIM
DN_CONV = 4
SWA_HEAD_DIM = 64
SWA_HEADS = 16
SWA_KV_HEADS = 2
SWA_GROUP = SWA_HEADS // SWA_KV_HEADS
SWA_KV_WIDTH = SWA_KV_HEADS * SWA_HEAD_DIM
WINDOW = 128
NUM_BUCKETS = 32
MAX_DISTANCE = 128
D_FF = 4 * D_MODEL

V7X_VMEM_BYTES = 64 * 1024 * 1024
VMEM_LIMIT_BYTES = V7X_VMEM_BYTES * 3 // 4
SUBLANES = 8
LANES = 128

ROW_TILE = 512
FF_CHUNK = 1024
CONV_COLS = 512
DELTA_CHUNK = 128


def _resident(shape):
    return pl.BlockSpec(shape, lambda *_: (0,) * len(shape), pipeline_mode=pl.Buffered(1))


def _params(*semantics):
    return pltpu.CompilerParams(dimension_semantics=semantics, vmem_limit_bytes=VMEM_LIMIT_BYTES)


def _rms_norm(x, w):
    return x * lax.rsqrt(jnp.mean(x * x, axis=-1, keepdims=True) + RMS_EPS) * w


def _sigmoid(x):
    return 1.0 / (1.0 + jnp.exp(-x))


def _dot(a, b):
    return jnp.dot(a, b, preferred_element_type=F32)


def _dot_bt(a, b):
    return lax.dot_general(a, b, (((1,), (1,)), ((), ())), preferred_element_type=F32)


def _dot_at(a, b):
    return lax.dot_general(a, b, (((0,), (0,)), ((), ())), preferred_element_type=F32)


def _out_mlp_kernel(x_ref, a_ref, wo_ref, bo_ref, nw_ref, wup_ref, wdn_ref, fw_ref, o_ref, *, final_norm):
    x1 = x_ref[...] + _dot(a_ref[...], wo_ref[...]) + bo_ref[...]
    h = _rms_norm(x1, nw_ref[...]).astype(BF16)
    acc = x1
    for c in range(D_FF // FF_CHUNK):
        cols = slice(c * FF_CHUNK, (c + 1) * FF_CHUNK)
        u = jnp.square(jnp.maximum(_dot(h, wup_ref[:, cols]), 0.0)).astype(BF16)
        acc = acc + _dot(u, wdn_ref[cols, :])
    if final_norm:
        acc = _rms_norm(acc, fw_ref[...])
    o_ref[...] = acc


def _out_mlp(x, a, w_out, b_out, norm_w, w_up, w_down, final_w, final_norm):
    m = x.shape[0]
    row = lambda i: (i, 0)
    return pl.pallas_call(
        functools.partial(_out_mlp_kernel, final_norm=final_norm),
        grid=(m // ROW_TILE,),
        in_specs=[
            pl.BlockSpec((ROW_TILE, D_MODEL), row),
            pl.BlockSpec((ROW_TILE, D_MODEL), row),
            _resident((D_MODEL, D_MODEL)),
            _resident((1, D_MODEL)),
            _resident((1, D_MODEL)),
            _resident((D_MODEL, D_FF)),
            _resident((D_FF, D_MODEL)),
            _resident((1, D_MODEL)),
        ],
        out_specs=pl.BlockSpec((ROW_TILE, D_MODEL), row),
        out_shape=jax.ShapeDtypeStruct((m, D_MODEL), F32),
        compiler_params=_params("parallel"),
        name="out_mlp",
    )(x, a, w_out, b_out, norm_w, w_up, w_down, final_w)


def _swa_qkv_kernel(x_ref, nw_ref, w_ref, b_ref, q_ref, kv_ref):
    h = _rms_norm(x_ref[...], nw_ref[...]).astype(BF16)
    y = _dot(h, w_ref[...]) + b_ref[...]
    q_ref[...] = (y[:, :D_MODEL] * (SWA_HEAD_DIM ** -0.5)).astype(BF16)
    kv_ref[...] = y[:, D_MODEL:].astype(BF16)


def _swa_qkv(x, norm_w, w_qkv, b_qkv):
    m = x.shape[0]
    n_out = D_MODEL + 2 * SWA_KV_WIDTH
    row = lambda i: (i, 0)
    return pl.pallas_call(
        _swa_qkv_kernel,
        grid=(m // ROW_TILE,),
        in_specs=[
            pl.BlockSpec((ROW_TILE, D_MODEL), row),
            _resident((1, D_MODEL)),
            _resident((D_MODEL, n_out)),
            _resident((1, n_out)),
        ],
        out_specs=[
            pl.BlockSpec((ROW_TILE, D_MODEL), row),
            pl.BlockSpec((ROW_TILE, 2 * SWA_KV_WIDTH), row),
        ],
        out_shape=[
            jax.ShapeDtypeStruct((m, D_MODEL), BF16),
            jax.ShapeDtypeStruct((m, 2 * SWA_KV_WIDTH), BF16),
        ],
        compiler_params=_params("parallel"),
        name="swa_qkv",
    )(x, norm_w, w_qkv, b_qkv)


def _t5_bucket_table():
    qi = np.arange(WINDOW)[:, None]
    kj = np.arange(2 * WINDOW)[None, :]
    dist = qi + WINDOW - kj
    n = np.maximum(dist, 0)
    max_exact = NUM_BUCKETS // 2
    ratio = np.log(np.maximum(n, 1).astype(np.float32) / np.float32(max_exact)) / np.float32(
        math.log(MAX_DISTANCE / max_exact))
    large = max_exact + (ratio * np.float32(NUM_BUCKETS - max_exact)).astype(np.int32)
    large = np.minimum(large, NUM_BUCKETS - 1)
    bucket = np.where(n < max_exact, n, large)
    valid = (dist >= 0) & (dist < WINDOW)
    return np.where(valid, bucket, -1).astype(np.int32)


def _t5_bias_kernel(bucket_ref, rel_ref, o_ref):
    head = pl.program_id(0)
    bucket = bucket_ref[...]
    acc = jnp.full(bucket.shape, -jnp.inf, F32)
    for b in range(NUM_BUCKETS):
        acc = jnp.where(bucket == b, rel_ref[b, head], acc)
    o_ref[0] = acc


def _t5_bias(rel_bias):
    shape = (WINDOW, 2 * WINDOW)
    return pl.pallas_call(
        _t5_bias_kernel,
        grid=(SWA_HEADS,),
        in_specs=[
            pl.BlockSpec(shape, lambda h: (0, 0)),
            pl.BlockSpec(memory_space=pltpu.SMEM),
        ],
        out_specs=pl.BlockSpec((1,) + shape, lambda h: (h, 0, 0)),
        out_shape=jax.ShapeDtypeStruct((SWA_HEADS,) + shape, F32),
        compiler_params=_params("parallel"),
        name="t5_bias",
    )(jnp.asarray(_t5_bucket_table()), rel_bias)


def _swa_attn_kernel(q_ref, kvc_ref, kvp_ref, bias_ref, sink_ref, o_ref):
    first_block = pl.program_id(1) == 0
    hd = SWA_HEAD_DIM
    keys = jnp.concatenate([kvp_ref[:, :SWA_KV_WIDTH], kvc_ref[:, :SWA_KV_WIDTH]], axis=0)
    vals = jnp.concatenate([kvp_ref[:, SWA_KV_WIDTH:], kvc_ref[:, SWA_KV_WIDTH:]], axis=0)
    col = lax.broadcasted_iota(jnp.int32, (WINDOW, 2 * WINDOW), 1)
    no_key = jnp.logical_and(first_block, col < WINDOW)
    zeros = jnp.zeros((2 * WINDOW, hd), BF16)

    for g in range(SWA_KV_HEADS):
        k_g = keys[:, g * hd:(g + 1) * hd]
        v_g = vals[:, g * hd:(g + 1) * hd]
        k_pad = (jnp.concatenate([k_g, zeros], axis=1), jnp.concatenate([zeros, k_g], axis=1))
        v_pad = (jnp.concatenate([v_g, zeros], axis=1), jnp.concatenate([zeros, v_g], axis=1))
        for j in range(SWA_GROUP // 2):
            pair = g * (SWA_GROUP // 2) + j
            lanes = slice(pair * 2 * hd, (pair + 1) * 2 * hd)
            q_pair = q_ref[:, lanes]
            out = None
            for e in range(2):
                head = 2 * pair + e
                s = _dot_bt(q_pair, k_pad[e]) + bias_ref[head]
                s = jnp.where(no_key, -jnp.inf, s)
                sink = sink_ref[head]
                m = jnp.maximum(jnp.max(s, axis=-1, keepdims=True), sink)
                p = jnp.exp(s - m)
                denom = jnp.sum(p, axis=-1, keepdims=True) + jnp.exp(sink - m)
                o = _dot(p.astype(BF16), v_pad[e]) / denom
                out = o if out is None else out + o
            o_ref[:, lanes] = out.astype(BF16)


def _swa_attn(q, kv, bias, sinks):
    b, s, _ = q.shape
    return pl.pallas_call(
        _swa_attn_kernel,
        grid=(b, s // WINDOW),
        in_specs=[
            pl.BlockSpec((None, WINDOW, D_MODEL), lambda i, n: (i, n, 0)),
            pl.BlockSpec((None, WINDOW, 2 * SWA_KV_WIDTH), lambda i, n: (i, n, 0)),
            pl.BlockSpec((None, WINDOW, 2 * SWA_KV_WIDTH), lambda i, n: (i, jnp.maximum(n - 1, 0), 0)),
            _resident((SWA_HEADS, WINDOW, 2 * WINDOW)),
            pl.BlockSpec(memory_space=pltpu.SMEM),
        ],
        out_specs=pl.BlockSpec((None, WINDOW, D_MODEL), lambda i, n: (i, n, 0)),
        out_shape=jax.ShapeDtypeStruct((b, s, D_MODEL), BF16),
        compiler_params=_params("parallel", "parallel"),
        name="swa_attn",
    )(q, kv, kv, bias, sinks)


def _dn_in_kernel(x_ref, nw_ref, w_ref, wba_ref, cw_ref, alog_ref, dt_ref, qkv_ref, z_ref, bg_ref,
                  halo_ref, pbuf_ref):
    tm = ROW_TILE

    @pl.when(pl.program_id(1) == 0)
    def _():
        halo_ref[...] = jnp.zeros_like(halo_ref)

    h = _rms_norm(x_ref[...], nw_ref[...]).astype(BF16)

    for c in range(3 * DN_WIDTH // CONV_COLS):
        c0 = c * CONV_COLS
        cols = slice(c0, c0 + CONV_COLS)
        p = _dot(h, w_ref[:, cols])
        pbuf_ref[0:SUBLANES, :] = halo_ref[:, cols]
        pbuf_ref[SUBLANES:SUBLANES + tm, :] = p
        halo_ref[:, cols] = p[tm - SUBLANES:, :]
        y = cw_ref[DN_CONV - 1:DN_CONV, cols] * p
        for j in range(DN_CONV - 1):
            start = SUBLANES - (DN_CONV - 1) + j
            y = y + cw_ref[j:j + 1, cols] * pbuf_ref[start:start + tm, :]
        y = y * _sigmoid(y)
        if c0 < 2 * DN_WIDTH:
            scale = DN_HEAD_DIM ** -0.5 if c0 < DN_WIDTH else 1.0
            for hh in range(CONV_COLS // DN_HEAD_DIM):
                t = y[:, hh * DN_HEAD_DIM:(hh + 1) * DN_HEAD_DIM]
                r = lax.rsqrt(jnp.sum(t * t, axis=-1, keepdims=True) + RMS_EPS) * scale
                qkv_ref[:, c0 + hh * DN_HEAD_DIM:c0 + (hh + 1) * DN_HEAD_DIM] = (t * r).astype(BF16)
        else:
            qkv_ref[:, cols] = y.astype(BF16)

    z_ref[...] = _dot(h, w_ref[:, 3 * DN_WIDTH:]).astype(BF16)

    pb = _dot(h, wba_ref[...])
    beta = _sigmoid(pb)
    t = pb + dt_ref[...]
    softplus = jnp.maximum(t, 0.0) + jnp.log(1.0 + jnp.exp(-jnp.abs(t)))
    g = -jnp.exp(alog_ref[...]) * softplus
    ri = lax.broadcasted_iota(jnp.int32, (DELTA_CHUNK, DELTA_CHUNK), 0)
    ci = lax.broadcasted_iota(jnp.int32, (DELTA_CHUNK, DELTA_CHUNK), 1)
    tri = (ri >= ci).astype(F32)
    lane = lax.broadcasted_iota(jnp.int32, (DELTA_CHUNK, LANES), 1)
    for c in range(tm // DELTA_CHUNK):
        rows = slice(c * DELTA_CHUNK, (c + 1) * DELTA_CHUNK)
        gc = jnp.dot(tri, g[rows], preferred_element_type=F32, precision=lax.Precision.HIGHEST)
        bg_ref[rows, :] = jnp.where(lane < DN_HEADS, beta[rows], gc)[:, :2 * DN_HEADS]


def _dn_in(x, norm_w, w_qkvz, w_ba, conv_w, alog_row, dt_row):
    b, s, _ = x.shape
    tok = lambda i, n: (i, n, 0)
    return pl.pallas_call(
        _dn_in_kernel,
        grid=(b, s // ROW_TILE),
        in_specs=[
            pl.BlockSpec((None, ROW_TILE, D_MODEL), tok),
            _resident((1, D_MODEL)),
            _resident((D_MODEL, 4 * DN_WIDTH)),
            _resident((D_MODEL, LANES)),
            _resident((DN_CONV, 3 * DN_WIDTH)),
            _resident((1, LANES)),
            _resident((1, LANES)),
        ],
        out_specs=[
            pl.BlockSpec((None, ROW_TILE, 3 * DN_WIDTH), tok),
            pl.BlockSpec((None, ROW_TILE, DN_WIDTH), tok),
            pl.BlockSpec((None, ROW_TILE, 2 * DN_HEADS), tok),
        ],
        out_shape=[
            jax.ShapeDtypeStruct((b, s, 3 * DN_WIDTH), BF16),
            jax.ShapeDtypeStruct((b, s, DN_WIDTH), BF16),
            jax.ShapeDtypeStruct((b, s, 2 * DN_HEADS), F32),
        ],
        scratch_shapes=[
            pltpu.VMEM((SUBLANES, 3 * DN_WIDTH), F32),
            pltpu.VMEM((SUBLANES + ROW_TILE, CONV_COLS), F32),
        ],
        compiler_params=_params("arbitrary", "arbitrary"),
        name="dn_in",
    )(x, norm_w, w_qkvz, w_ba, conv_w, alog_row, dt_row)


def _delta_kernel(qkv_ref, z_ref, bg_ref, bgt_ref, nw_ref, o_ref, state_ref):
    c_len, d = DELTA_CHUNK, DN_HEAD_DIM
    batch = qkv_ref.shape[0]

    @pl.when(pl.program_id(0) == 0)
    def _():
        state_ref[...] = jnp.zeros_like(state_ref)

    ri = lax.broadcasted_iota(jnp.int32, (c_len, c_len), 0)
    ci = lax.broadcasted_iota(jnp.int32, (c_len, c_len), 1)
    causal = ri >= ci
    strict = ri > ci
    eye = (ri == ci).astype(F32)
    sub_blocks = [
        ((ri >> (k + 1)) == (ci >> (k + 1))) & (((ri >> k) & 1) == 1) & (((ci >> k) & 1) == 0)
        for k in range(int(math.log2(c_len)))
    ]
    norm_w = nw_ref[...]

    def operands(i, h):
        q = qkv_ref[i, :, h * d:(h + 1) * d]
        k = qkv_ref[i, :, DN_WIDTH + h * d:DN_WIDTH + (h + 1) * d]
        v = qkv_ref[i, :, 2 * DN_WIDTH + h * d:2 * DN_WIDTH + (h + 1) * d]
        beta = bg_ref[i, :, h:h + 1]
        gc_col = bg_ref[i, :, DN_HEADS + h:DN_HEADS + h + 1]
        gc_row = bgt_ref[i, DN_HEADS + h:DN_HEADS + h + 1, :]
        return q, k, v, beta, gc_col, gc_row

    problems = [(i, h) for i in range(batch) for h in range(DN_HEADS)]

    attn, neg_lower = {}, {}
    for p in problems:
        q, k, _, beta, gc_col, gc_row = operands(*p)
        k_beta = (k.astype(F32) * beta).astype(BF16)
        decay = jnp.exp(jnp.where(causal, gc_col - gc_row, -jnp.inf))
        qk = _dot_bt(jnp.concatenate([q, k_beta], axis=0), k)
        attn[p] = (qk[:c_len] * decay).astype(BF16)
        neg_lower[p] = jnp.where(strict, -(qk[c_len:] * decay), 0.0)

    inv = {p: eye + jnp.where(sub_blocks[0], neg_lower[p], 0.0) for p in problems}
    for mask in sub_blocks[1:]:
        n_t = {p: _dot(jnp.where(mask, neg_lower[p], 0.0).astype(BF16), inv[p].astype(BF16)) for p in problems}
        inv = {p: inv[p] + _dot(inv[p].astype(BF16), n_t[p].astype(BF16)) for p in problems}

    sol = {}
    for p in problems:
        _, k, v, beta, gc_col, _ = operands(*p)
        k_beta = k.astype(F32) * beta
        rhs = jnp.concatenate([v.astype(F32) * beta, k_beta * jnp.exp(gc_col)], axis=1).astype(BF16)
        sol[p] = _dot(inv[p].astype(BF16), rhs)

    ws_qs = {}
    for p in problems:
        i, h = p
        q, _, _, _, gc_col, _ = operands(*p)
        q_dec = q.astype(F32) * jnp.exp(gc_col)
        w_q = jnp.concatenate([sol[p][:, d:], q_dec], axis=0).astype(BF16)
        ws_qs[p] = _dot(w_q, state_ref[i * DN_HEADS + h].astype(BF16))

    o_raw, kv = {}, {}
    for p in problems:
        _, k, _, _, gc_col, gc_row = operands(*p)
        g_last = gc_row[:, c_len - 1:]
        k_dec = (k.astype(F32) * jnp.exp(g_last - gc_col)).astype(BF16)
        v_new = (sol[p][:, :d] - ws_qs[p][:c_len]).astype(BF16)
        o_raw[p] = ws_qs[p][c_len:] + _dot(attn[p], v_new)
        kv[p] = _dot_at(k_dec, v_new)

    for p in problems:
        i, h = p
        lanes = slice(h * d, (h + 1) * d)
        g_last = bgt_ref[i, DN_HEADS + h:DN_HEADS + h + 1, c_len - 1:]
        state_ref[i * DN_HEADS + h] = state_ref[i * DN_HEADS + h] * jnp.exp(g_last) + kv[p]
        zf = z_ref[i, :, lanes].astype(F32)
        o = o_raw[p]
        o = o * lax.rsqrt(jnp.mean(o * o, axis=-1, keepdims=True) + RMS_EPS) * norm_w * (zf * _sigmoid(zf))
        o_ref[i, :, lanes] = o.astype(BF16)


def _delta_rule(qkv, z, bg, norm_w):
    b, s, _ = qkv.shape
    t = DELTA_CHUNK
    ns = s // t
    bg4 = bg.reshape(b, ns, t, 2 * DN_HEADS)
    bgt4 = jnp.swapaxes(bg4, 2, 3)
    tok = lambda n: (0, n, 0)
    tok4 = lambda n: (0, n, 0, 0)
    return pl.pallas_call(
        _delta_kernel,
        grid=(ns,),
        in_specs=[
            pl.BlockSpec((b, t, 3 * DN_WIDTH), tok),
            pl.BlockSpec((b, t, DN_WIDTH), tok),
            pl.BlockSpec((b, None, t, 2 * DN_HEADS), tok4),
            pl.BlockSpec((b, None, 2 * DN_HEADS, t), tok4),
            _resident((1, DN_HEAD_DIM)),
        ],
        out_specs=pl.BlockSpec((b, t, DN_WIDTH), tok),
        out_shape=jax.ShapeDtypeStruct((b, s, DN_WIDTH), BF16),
        scratch_shapes=[pltpu.VMEM((b * DN_HEADS, DN_HEAD_DIM, DN_HEAD_DIM), F32)],
        compiler_params=_params("arbitrary"),
        name="delta_rule",
    )(qkv, z, bg4, bgt4, norm_w)


def _row(v):
    return v.reshape(1, -1).astype(F32)


def _gate_row(v):
    return jnp.zeros((1, LANES), F32).at[0, DN_HEADS:2 * DN_HEADS].set(v.astype(F32))


def kernel(x, norm_mix, norm_mlp, norm_final, dn_w_in, dn_conv_w, dn_a_log, dn_dt_bias, dn_norm_w, dn_w_out,
           swa_w_qkv, swa_b_qkv, swa_sinks, swa_w_out, swa_b_out, rel_bias, mlp_w_up, mlp_w_down):
    b, s, d = x.shape
    depth = norm_mix.shape[0]
    m = b * s
    bias = _t5_bias(rel_bias.astype(F32))
    zero_bias = jnp.zeros((1, d), F32)
    final_w = _row(norm_final)
    for i in range(depth):
        j = i // 2
        if i % 2 == 0:
            w_in = dn_w_in[j]
            w_ba = jnp.zeros((d, LANES), BF16).at[:, :2 * DN_HEADS].set(w_in[:, 4 * DN_WIDTH:].astype(BF16))
            qkv, z, bg = _dn_in(x, _row(norm_mix[i]), w_in[:, :4 * DN_WIDTH].astype(BF16), w_ba,
                                dn_conv_w[j].astype(F32), _gate_row(dn_a_log[j]), _gate_row(dn_dt_bias[j]))
            mixed = _delta_rule(qkv, z, bg, _row(dn_norm_w[j]))
            w_out, b_out = dn_w_out[j], zero_bias
        else:
            q, kv = _swa_qkv(x.reshape(m, d), _row(norm_mix[i]), swa_w_qkv[j].astype(BF16), _row(swa_b_qkv[j]))
            mixed = _swa_attn(q.reshape(b, s, d), kv.reshape(b, s, 2 * SWA_KV_WIDTH), bias,
                              swa_sinks[j].astype(F32))
            w_out, b_out = swa_w_out[j], _row(swa_b_out[j])
        x = _out_mlp(x.reshape(m, d), mixed.reshape(m, d), w_out.astype(BF16), b_out, _row(norm_mlp[i]),
                     mlp_w_up[i].astype(BF16), mlp_w_down[i].astype(BF16), final_w,
                     final_norm=(i == depth - 1)).reshape(b, s, d)
    return x
```

```python
import functools
import math

import numpy as np
import jax
import jax.numpy as jnp
from jax import lax
from jax.experimental import pallas as pl
from jax.experimental.pallas import tpu as pltpu

F32 = jnp.float32
BF16 = jnp.bfloat16

D_MODEL = 1024
RMS_EPS = 1e-6
DN_HEAD_DIM = 128
DN_HEADS = 8
DN_WIDTH = DN_HEADS * DN_HEAD_DIM
DN_CONV = 4
SWA_HEAD_DIM = 64
SWA_HEADS = 16
SWA_KV_HEADS = 2
SWA_GROUP = SWA_HEADS // SWA_KV_HEADS
SWA_KV_WIDTH = SWA_KV_HEADS * SWA_HEAD_DIM
WINDOW = 128
NUM_BUCKETS = 32
MAX_DISTANCE = 128
LOG2_E = math.log2(math.e)
D_FF = 4 * D_MODEL

V7X_VMEM_BYTES = 64 * 1024 * 1024
VMEM_LIMIT_BYTES = V7X_VMEM_BYTES * 3 // 4
SUBLANES = 8
LANES = 128

ROW_TILE = 512
FF_CHUNK = 1024
CONV_COLS = 512
ATTN_BLOCKS = 2
DELTA_CHUNK = 128


def _resident(shape):
    return pl.BlockSpec(shape, lambda *_: (0,) * len(shape), pipeline_mode=pl.Buffered(1))


def _params(*semantics):
    return pltpu.CompilerParams(dimension_semantics=semantics, vmem_limit_bytes=VMEM_LIMIT_BYTES)


def _rms_norm(x, w):
    return x * lax.rsqrt(jnp.mean(x * x, axis=-1, keepdims=True) + RMS_EPS) * w


def _sigmoid(x):
    return 1.0 / (1.0 + jnp.exp(-x))


def _dot(a, b):
    return jnp.dot(a, b, preferred_element_type=F32)


def _dot_bt(a, b):
    return lax.dot_general(a, b, (((1,), (1,)), ((), ())), preferred_element_type=F32)


def _dot_at(a, b):
    return lax.dot_general(a, b, (((0,), (0,)), ((), ())), preferred_element_type=F32)


def _out_mlp_kernel(x_ref, a_ref, wo_ref, bo_ref, nw_ref, wup_ref, wdn_ref, fw_ref, o_ref, *, final_norm):
    x1 = x_ref[...] + _dot(a_ref[...], wo_ref[...]) + bo_ref[...]
    h = _rms_norm(x1, nw_ref[...]).astype(BF16)
    acc = x1
    for c in range(D_FF // FF_CHUNK):
        cols = slice(c * FF_CHUNK, (c + 1) * FF_CHUNK)
        u = jnp.square(jnp.maximum(_dot(h, wup_ref[:, cols]), 0.0)).astype(BF16)
        acc = acc + _dot(u, wdn_ref[cols, :])
    if final_norm:
        acc = _rms_norm(acc, fw_ref[...])
    o_ref[...] = acc


def _out_mlp(x, a, w_out, b_out, norm_w, w_up, w_down, final_w, final_norm):
    m = x.shape[0]
    row = lambda i: (i, 0)
    return pl.pallas_call(
        functools.partial(_out_mlp_kernel, final_norm=final_norm),
        grid=(m // ROW_TILE,),
        in_specs=[
            pl.BlockSpec((ROW_TILE, D_MODEL), row),
            pl.BlockSpec((ROW_TILE, D_MODEL), row),
            _resident((D_MODEL, D_MODEL)),
            _resident((1, D_MODEL)),
            _resident((1, D_MODEL)),
            _resident((D_MODEL, D_FF)),
            _resident((D_FF, D_MODEL)),
            _resident((1, D_MODEL)),
        ],
        out_specs=pl.BlockSpec((ROW_TILE, D_MODEL), row),
        out_shape=jax.ShapeDtypeStruct((m, D_MODEL), F32),
        compiler_params=_params("parallel"),
        name="out_mlp",
    )(x, a, w_out, b_out, norm_w, w_up, w_down, final_w)


def _swa_qkv_kernel(x_ref, nw_ref, w_ref, b_ref, q_ref, kv_ref):
    h = _rms_norm(x_ref[...], nw_ref[...]).astype(BF16)
    y = _dot(h, w_ref[...]) + b_ref[...]
    q_ref[...] = (y[:, :D_MODEL] * (SWA_HEAD_DIM ** -0.5 * LOG2_E)).astype(BF16)
    kv_ref[...] = y[:, D_MODEL:].astype(BF16)


def _swa_qkv(x, norm_w, w_qkv, b_qkv):
    m = x.shape[0]
    n_out = D_MODEL + 2 * SWA_KV_WIDTH
    row = lambda i: (i, 0)
    return pl.pallas_call(
        _swa_qkv_kernel,
        grid=(m // ROW_TILE,),
        in_specs=[
            pl.BlockSpec((ROW_TILE, D_MODEL), row),
            _resident((1, D_MODEL)),
            _resident((D_MODEL, n_out)),
            _resident((1, n_out)),
        ],
        out_specs=[
            pl.BlockSpec((ROW_TILE, D_MODEL), row),
            pl.BlockSpec((ROW_TILE, 2 * SWA_KV_WIDTH), row),
        ],
        out_shape=[
            jax.ShapeDtypeStruct((m, D_MODEL), BF16),
            jax.ShapeDtypeStruct((m, 2 * SWA_KV_WIDTH), BF16),
        ],
        compiler_params=_params("parallel"),
        name="swa_qkv",
    )(x, norm_w, w_qkv, b_qkv)


def _t5_bucket_table():
    qi = np.arange(WINDOW)[:, None]
    kj = np.arange(2 * WINDOW)[None, :]
    dist = qi + WINDOW - kj
    n = np.maximum(dist, 0)
    max_exact = NUM_BUCKETS // 2
    ratio = np.log(np.maximum(n, 1).astype(np.float32) / np.float32(max_exact)) / np.float32(
        math.log(MAX_DISTANCE / max_exact))
    large = max_exact + (ratio * np.float32(NUM_BUCKETS - max_exact)).astype(np.int32)
    large = np.minimum(large, NUM_BUCKETS - 1)
    bucket = np.where(n < max_exact, n, large)
    valid = (dist >= 0) & (dist < WINDOW)
    return np.where(valid, bucket, -1).astype(np.int32)


def _t5_bias_kernel(bucket_ref, rel_ref, o_ref):
    head = pl.program_id(0)
    bucket = bucket_ref[...]
    acc = jnp.full(bucket.shape, -jnp.inf, F32)
    for b in range(NUM_BUCKETS):
        acc = jnp.where(bucket == b, rel_ref[b, head] * LOG2_E, acc)
    o_ref[0, 0] = acc
    col = lax.broadcasted_iota(jnp.int32, bucket.shape, 1)
    o_ref[1, 0] = jnp.where(col < WINDOW, -jnp.inf, acc)


def _t5_bias(rel_bias):
    shape = (WINDOW, 2 * WINDOW)
    return pl.pallas_call(
        _t5_bias_kernel,
        grid=(SWA_HEADS,),
        in_specs=[
            pl.BlockSpec(shape, lambda h: (0, 0)),
            pl.BlockSpec(memory_space=pltpu.SMEM),
        ],
        out_specs=pl.BlockSpec((2, 1) + shape, lambda h: (0, h, 0, 0)),
        out_shape=jax.ShapeDtypeStruct((2, SWA_HEADS) + shape, F32),
        compiler_params=_params("parallel"),
        name="t5_bias",
    )(jnp.asarray(_t5_bucket_table()), rel_bias)


def _swa_attn_kernel(q_ref, kvc_ref, kvp_ref, bias_ref, sink_ref, o_ref):
    hd = SWA_HEAD_DIM
    first_table = jnp.where(pl.program_id(1) == 0, 1, 0)
    zeros = jnp.zeros((2 * WINDOW, hd), BF16)

    for blk in range(ATTN_BLOCKS):
        rows = slice(blk * WINDOW, (blk + 1) * WINDOW)
        if blk == 0:
            kv = jnp.concatenate([kvp_ref[...], kvc_ref[:WINDOW, :]], axis=0)
            table = first_table
        else:
            kv = kvc_ref[(blk - 1) * WINDOW:(blk + 1) * WINDOW, :]
            table = 0
        for g in range(SWA_KV_HEADS):
            k_g = kv[:, g * hd:(g + 1) * hd]
            v_g = kv[:, SWA_KV_WIDTH + g * hd:SWA_KV_WIDTH + (g + 1) * hd]
            k_pad = (jnp.concatenate([k_g, zeros], axis=1), jnp.concatenate([zeros, k_g], axis=1))
            v_pad = (jnp.concatenate([v_g, zeros], axis=1), jnp.concatenate([zeros, v_g], axis=1))
            for j in range(SWA_GROUP // 2):
                pair = g * (SWA_GROUP // 2) + j
                lanes = slice(pair * 2 * hd, (pair + 1) * 2 * hd)
                q_pair = q_ref[rows, lanes]
                out = None
                for e in range(2):
                    head = 2 * pair + e
                    s = _dot_bt(q_pair, k_pad[e]) + bias_ref[table, head]
                    sink = sink_ref[head] * LOG2_E
                    m = jnp.maximum(jnp.max(s, axis=-1, keepdims=True), sink)
                    p = jnp.exp2(s - m)
                    denom = jnp.sum(p, axis=-1, keepdims=True) + jnp.exp2(sink - m)
                    o = _dot(p.astype(BF16), v_pad[e]) / denom
                    out = o if out is None else out + o
                o_ref[rows, lanes] = out.astype(BF16)


def _swa_attn(q, kv, bias, sinks):
    b, s, _ = q.shape
    t = ATTN_BLOCKS * WINDOW
    return pl.pallas_call(
        _swa_attn_kernel,
        grid=(b, s // t),
        in_specs=[
            pl.BlockSpec((None, t, D_MODEL), lambda i, n: (i, n, 0)),
            pl.BlockSpec((None, t, 2 * SWA_KV_WIDTH), lambda i, n: (i, n, 0)),
            pl.BlockSpec((None, WINDOW, 2 * SWA_KV_WIDTH),
                         lambda i, n: (i, jnp.maximum(n * ATTN_BLOCKS - 1, 0), 0)),
            _resident((2, SWA_HEADS, WINDOW, 2 * WINDOW)),
            pl.BlockSpec(memory_space=pltpu.SMEM),
        ],
        out_specs=pl.BlockSpec((None, t, D_MODEL), lambda i, n: (i, n, 0)),
        out_shape=jax.ShapeDtypeStruct((b, s, D_MODEL), BF16),
        compiler_params=_params("parallel", "parallel"),
        name="swa_attn",
    )(q, kv, kv, bias, sinks)


def _dn_in_kernel(x_ref, nw_ref, w_ref, wba_ref, cw_ref, alog_ref, dt_ref, qkv_ref, z_ref, bg_ref,
                  halo_ref, pbuf_ref):
    tm = ROW_TILE

    @pl.when(pl.program_id(1) == 0)
    def _():
        halo_ref[...] = jnp.zeros_like(halo_ref)

    h = _rms_norm(x_ref[...], nw_ref[...]).astype(BF16)

    for c in range(3 * DN_WIDTH // CONV_COLS):
        c0 = c * CONV_COLS
        cols = slice(c0, c0 + CONV_COLS)
        p = _dot(h, w_ref[:, cols])
        pbuf_ref[0:SUBLANES, :] = halo_ref[:, cols]
        pbuf_ref[SUBLANES:SUBLANES + tm, :] = p
        halo_ref[:, cols] = p[tm - SUBLANES:, :]
        y = cw_ref[DN_CONV - 1:DN_CONV, cols] * p
        for j in range(DN_CONV - 1):
            start = SUBLANES - (DN_CONV - 1) + j
            y = y + cw_ref[j:j + 1, cols] * pbuf_ref[start:start + tm, :]
        y = y * _sigmoid(y)
        if c0 < 2 * DN_WIDTH:
            scale = DN_HEAD_DIM ** -0.5 if c0 < DN_WIDTH else 1.0
            for hh in range(CONV_COLS // DN_HEAD_DIM):
                t = y[:, hh * DN_HEAD_DIM:(hh + 1) * DN_HEAD_DIM]
                r = lax.rsqrt(jnp.sum(t * t, axis=-1, keepdims=True) + RMS_EPS) * scale
                qkv_ref[:, c0 + hh * DN_HEAD_DIM:c0 + (hh + 1) * DN_HEAD_DIM] = (t * r).astype(BF16)
        else:
            qkv_ref[:, cols] = y.astype(BF16)

    z_ref[...] = _dot(h, w_ref[:, 3 * DN_WIDTH:]).astype(BF16)

    pb = _dot(h, wba_ref[...])
    beta = _sigmoid(pb)
    t = pb + dt_ref[...]
    softplus = jnp.maximum(t, 0.0) + jnp.log(1.0 + jnp.exp(-jnp.abs(t)))
    g = -jnp.exp(alog_ref[...]) * softplus
    ri = lax.broadcasted_iota(jnp.int32, (DELTA_CHUNK, DELTA_CHUNK), 0)
    ci = lax.broadcasted_iota(jnp.int32, (DELTA_CHUNK, DELTA_CHUNK), 1)
    tri = (ri >= ci).astype(F32)
    lane = lax.broadcasted_iota(jnp.int32, (DELTA_CHUNK, LANES), 1)
    for c in range(tm // DELTA_CHUNK):
        rows = slice(c * DELTA_CHUNK, (c + 1) * DELTA_CHUNK)
        gc = jnp.dot(tri, g[rows], preferred_element_type=F32, precision=lax.Precision.HIGHEST)
        bg_ref[rows, :] = jnp.where(lane < DN_HEADS, beta[rows], gc)[:, :2 * DN_HEADS]


def _dn_in(x, norm_w, w_qkvz, w_ba, conv_w, alog_row, dt_row):
    b, s, _ = x.shape
    tok = lambda i, n: (i, n, 0)
    return pl.pallas_call(
        _dn_in_kernel,
        grid=(b, s // ROW_TILE),
        in_specs=[
            pl.BlockSpec((None, ROW_TILE, D_MODEL), tok),
            _resident((1, D_MODEL)),
            _resident((D_MODEL, 4 * DN_WIDTH)),
            _resident((D_MODEL, LANES)),
            _resident((DN_CONV, 3 * DN_WIDTH)),
            _resident((1, LANES)),
            _resident((1, LANES)),
        ],
        out_specs=[
            pl.BlockSpec((None, ROW_TILE, 3 * DN_WIDTH), tok),
            pl.BlockSpec((None, ROW_TILE, DN_WIDTH), tok),
            pl.BlockSpec((None, ROW_TILE, 2 * DN_HEADS), tok),
        ],
        out_shape=[
            jax.ShapeDtypeStruct((b, s, 3 * DN_WIDTH), BF16),
            jax.ShapeDtypeStruct((b, s, DN_WIDTH), BF16),
            jax.ShapeDtypeStruct((b, s, 2 * DN_HEADS), F32),
        ],
        scratch_shapes=[
            pltpu.VMEM((SUBLANES, 3 * DN_WIDTH), F32),
            pltpu.VMEM((SUBLANES + ROW_TILE, CONV_COLS), F32),
        ],
        compiler_params=_params("arbitrary", "arbitrary"),
        name="dn_in",
    )(x, norm_w, w_qkvz, w_ba, conv_w, alog_row, dt_row)


def _delta_kernel(qkv_ref, z_ref, bg_ref, bgt_ref, nw_ref, o_ref, state_ref):
    c_len, d = DELTA_CHUNK, DN_HEAD_DIM
    batch = qkv_ref.shape[0]

    @pl.when(pl.program_id(0) == 0)
    def _():
        state_ref[...] = jnp.zeros_like(state_ref)

    ri = lax.broadcasted_iota(jnp.int32, (c_len, c_len), 0)
    ci = lax.broadcasted_iota(jnp.int32, (c_len, c_len), 1)
    causal = ri >= ci
    strict = ri > ci
    eye = (ri == ci).astype(F32)
    sub_blocks = [
        ((ri >> (k + 1)) == (ci >> (k + 1))) & (((ri >> k) & 1) == 1) & (((ci >> k) & 1) == 0)
        for k in range(int(math.log2(c_len)))
    ]
    norm_w = nw_ref[...]

    def operands(i, h):
        q = qkv_ref[i, :, h * d:(h + 1) * d]
        k = qkv_ref[i, :, DN_WIDTH + h * d:DN_WIDTH + (h + 1) * d]
        v = qkv_ref[i, :, 2 * DN_WIDTH + h * d:2 * DN_WIDTH + (h + 1) * d]
        return q, k, v

    def gates(i, h):
        beta_col = bg_ref[i, :, h:h + 1]
        gc_col = bg_ref[i, :, DN_HEADS + h:DN_HEADS + h + 1]
        beta_row = bgt_ref[i, h:h + 1, :]
        gc_row = bgt_ref[i, DN_HEADS + h:DN_HEADS + h + 1, :]
        return beta_col, gc_col, beta_row, gc_row

    problems = [(i, h) for i in range(batch) for h in range(DN_HEADS)]

    attn, neg_lower = {}, {}
    for p in problems:
        q, k, _ = operands(*p)
        beta_col, gc_col, _, gc_row = gates(*p)
        decay = jnp.exp(jnp.where(causal, gc_col - gc_row, -jnp.inf))
        qk = _dot_bt(jnp.concatenate([q, k], axis=0), k)
        attn[p] = (qk[:c_len] * decay).astype(BF16)
        neg_lower[p] = jnp.where(strict, -(qk[c_len:] * (decay * beta_col)), 0.0).astype(BF16)

    inv = {p: eye + jnp.where(sub_blocks[0], neg_lower[p].astype(F32), 0.0) for p in problems}
    for mask in sub_blocks[1:]:
        inv_b = {p: inv[p].astype(BF16) for p in problems}
        n_t = {p: jnp.where(mask, _dot(neg_lower[p], inv_b[p]), 0.0).astype(BF16) for p in problems}
        inv = {p: inv[p] + _dot(inv_b[p], n_t[p]) for p in problems}

    u, w = {}, {}
    for p in problems:
        _, k, v = operands(*p)
        _, _, beta_row, gc_row = gates(*p)
        u[p] = _dot((inv[p] * beta_row).astype(BF16), v)
        w[p] = _dot((inv[p] * (beta_row * jnp.exp(gc_row))).astype(BF16), k).astype(BF16)

    ws_qs = {}
    for p in problems:
        i, h = p
        q, _, _ = operands(*p)
        ws_qs[p] = _dot(jnp.concatenate([w[p], q], axis=0), state_ref[i * DN_HEADS + h].astype(BF16))

    o_raw, kv = {}, {}
    for p in problems:
        _, k, _ = operands(*p)
        _, gc_col, _, gc_row = gates(*p)
        g_last = gc_row[:, c_len - 1:]
        v_new = u[p] - ws_qs[p][:c_len]
        o_raw[p] = ws_qs[p][c_len:] * jnp.exp(gc_col) + _dot(attn[p], v_new.astype(BF16))
        kv[p] = _dot_at(k, (v_new * jnp.exp(g_last - gc_col)).astype(BF16))

    for p in problems:
        i, h = p
        lanes = slice(h * d, (h + 1) * d)
        g_last = bgt_ref[i, DN_HEADS + h:DN_HEADS + h + 1, c_len - 1:]
        state_ref[i * DN_HEADS + h] = state_ref[i * DN_HEADS + h] * jnp.exp(g_last) + kv[p]
        zf = z_ref[i, :, lanes].astype(F32)
        o = o_raw[p]
        o = o * lax.rsqrt(jnp.mean(o * o, axis=-1, keepdims=True) + RMS_EPS) * norm_w * (zf * _sigmoid(zf))
        o_ref[i, :, lanes] = o.astype(BF16)


def _delta_rule(qkv, z, bg, norm_w):
    b, s, _ = qkv.shape
    t = DELTA_CHUNK
    ns = s // t
    bg4 = bg.reshape(b, ns, t, 2 * DN_HEADS)
    bgt4 = jnp.swapaxes(bg4, 2, 3)
    tok = lambda n: (0, n, 0)
    tok4 = lambda n: (0, n, 0, 0)
    return pl.pallas_call(
        _delta_kernel,
        grid=(ns,),
        in_specs=[
            pl.BlockSpec((b, t, 3 * DN_WIDTH), tok),
            pl.BlockSpec((b, t, DN_WIDTH), tok),
            pl.BlockSpec((b, None, t, 2 * DN_HEADS), tok4),
            pl.BlockSpec((b, None, 2 * DN_HEADS, t), tok4),
            _resident((1, DN_HEAD_DIM)),
        ],
        out_specs=pl.BlockSpec((b, t, DN_WIDTH), tok),
        out_shape=jax.ShapeDtypeStruct((b, s, DN_WIDTH), BF16),
        scratch_shapes=[pltpu.VMEM((b * DN_HEADS, DN_HEAD_DIM, DN_HEAD_DIM), F32)],
        compiler_params=_params("arbitrary"),
        name="delta_rule",
    )(qkv, z, bg4, bgt4, norm_w)


def _row(v):
    return v.reshape(1, -1).astype(F32)


def _gate_row(v):
    return jnp.zeros((1, LANES), F32).at[0, DN_HEADS:2 * DN_HEADS].set(v.astype(F32))


def kernel(x, norm_mix, norm_mlp, norm_final, dn_w_in, dn_conv_w, dn_a_log, dn_dt_bias, dn_norm_w, dn_w_out,
           swa_w_qkv, swa_b_qkv, swa_sinks, swa_w_out, swa_b_out, rel_bias, mlp_w_up, mlp_w_down):
    b, s, d = x.shape
    depth = norm_mix.shape[0]
    m = b * s
    bias = _t5_bias(rel_bias.astype(F32))
    zero_bias = jnp.zeros((1, d), F32)
    final_w = _row(norm_final)
    for i in range(depth):
        j = i // 2
        if i % 2 == 0:
            w_in = dn_w_in[j]
            w_ba = jnp.zeros((d, LANES), BF16).at[:, :2 * DN_HEADS].set(w_in[:, 4 * DN_WIDTH:].astype(BF16))
            qkv, z, bg = _dn_in(x, _row(norm_mix[i]), w_in[:, :4 * DN_WIDTH].astype(BF16), w_ba,
                                dn_conv_w[j].astype(F32), _gate_row(dn_a_log[j]), _gate_row(dn_dt_bias[j]))
            mixed = _delta_rule(qkv, z, bg, _row(dn_norm_w[j]))
            w_out, b_out = dn_w_out[j], zero_bias
        else:
            q, kv = _swa_qkv(x.reshape(m, d), _row(norm_mix[i]), swa_w_qkv[j].astype(BF16), _row(swa_b_qkv[j]))
            mixed = _swa_attn(q.reshape(b, s, d), kv.reshape(b, s, 2 * SWA_KV_WIDTH), bias,
                              swa_sinks[j].astype(F32))
            w_out, b_out = swa_w_out[j], _row(swa_b_out[j])
        x = _out_mlp(x.reshape(m, d), mixed.reshape(m, d), w_out.astype(BF16), b_out, _row(norm_mlp[i]),
                     mlp_w_up[i].astype(BF16), mlp_w_down[i].astype(BF16), final_w,
                     final_norm=(i == depth - 1)).reshape(b, s, d)
    return x
```

```python
import functools
import math

import numpy as np
import jax
import jax.numpy as jnp
from jax import lax
from jax.experimental import pallas as pl
from jax.experimental.pallas import tpu as pltpu

F32 = jnp.float32
BF16 = jnp.bfloat16

D_MODEL = 1024
RMS_EPS = 1e-6
DN_HEAD_DIM = 128
DN_HEADS = 8
DN_WIDTH = DN_HEADS * DN_HEAD_DIM
DN_CONV = 4
SWA_HEAD_DIM = 64
SWA_HEADS = 16
SWA_KV_HEADS = 2
SWA_GROUP = SWA_HEADS // SWA_KV_HEADS
SWA_KV_WIDTH = SWA_KV_HEADS * SWA_HEAD_DIM
WINDOW = 128
NUM_BUCKETS = 32
MAX_DISTANCE = 128
LOG2_E = math.log2(math.e)
D_FF = 4 * D_MODEL

V7X_VMEM_BYTES = 64 * 1024 * 1024
VMEM_LIMIT_BYTES = V7X_VMEM_BYTES * 3 // 4
SUBLANES = 8
LANES = 128

ROW_TILE = 512
FF_CHUNK = 1024
CONV_COLS = 512
ATTN_BLOCKS = 4
DELTA_CHUNK = 128


def _resident(shape):
    return pl.BlockSpec(shape, lambda *_: (0,) * len(shape), pipeline_mode=pl.Buffered(1))


def _params(*semantics):
    return pltpu.CompilerParams(dimension_semantics=semantics, vmem_limit_bytes=VMEM_LIMIT_BYTES)


def _rms_norm(x, w):
    return x * lax.rsqrt(jnp.mean(x * x, axis=-1, keepdims=True) + RMS_EPS) * w


def _sigmoid(x):
    return 1.0 / (1.0 + jnp.exp(-x))


def _dot(a, b):
    return jnp.dot(a, b, preferred_element_type=F32)


def _dot_bt(a, b):
    return lax.dot_general(a, b, (((1,), (1,)), ((), ())), preferred_element_type=F32)


def _dot_at(a, b):
    return lax.dot_general(a, b, (((0,), (0,)), ((), ())), preferred_element_type=F32)


def _out_mlp_kernel(x_ref, a_ref, wo_ref, bo_ref, nw_ref, wup_ref, wdn_ref, fw_ref, o_ref, *, final_norm):
    x1 = x_ref[...] + _dot(a_ref[...], wo_ref[...]) + bo_ref[...]
    h = _rms_norm(x1, nw_ref[...]).astype(BF16)
    acc = x1
    for c in range(D_FF // FF_CHUNK):
        cols = slice(c * FF_CHUNK, (c + 1) * FF_CHUNK)
        u = jnp.square(jnp.maximum(_dot(h, wup_ref[:, cols]), 0.0)).astype(BF16)
        acc = acc + _dot(u, wdn_ref[cols, :])
    if final_norm:
        acc = _rms_norm(acc, fw_ref[...])
    o_ref[...] = acc


def _out_mlp(x, a, w_out, b_out, norm_w, w_up, w_down, final_w, final_norm):
    m = x.shape[0]
    row = lambda i: (i, 0)
    return pl.pallas_call(
        functools.partial(_out_mlp_kernel, final_norm=final_norm),
        grid=(m // ROW_TILE,),
        in_specs=[
            pl.BlockSpec((ROW_TILE, D_MODEL), row),
            pl.BlockSpec((ROW_TILE, D_MODEL), row),
            _resident((D_MODEL, D_MODEL)),
            _resident((1, D_MODEL)),
            _resident((1, D_MODEL)),
            _resident((D_MODEL, D_FF)),
            _resident((D_FF, D_MODEL)),
            _resident((1, D_MODEL)),
        ],
        out_specs=pl.BlockSpec((ROW_TILE, D_MODEL), row),
        out_shape=jax.ShapeDtypeStruct((m, D_MODEL), F32),
        compiler_params=_params("parallel"),
        name="out_mlp",
    )(x, a, w_out, b_out, norm_w, w_up, w_down, final_w)


def _swa_qkv_kernel(x_ref, nw_ref, w_ref, b_ref, q_ref, kv_ref):
    h = _rms_norm(x_ref[...], nw_ref[...]).astype(BF16)
    y = _dot(h, w_ref[...]) + b_ref[...]
    q_ref[...] = (y[:, :D_MODEL] * (SWA_HEAD_DIM ** -0.5 * LOG2_E)).astype(BF16)
    kv_ref[...] = y[:, D_MODEL:].astype(BF16)


def _swa_qkv(x, norm_w, w_qkv, b_qkv):
    m = x.shape[0]
    n_out = D_MODEL + 2 * SWA_KV_WIDTH
    row = lambda i: (i, 0)
    return pl.pallas_call(
        _swa_qkv_kernel,
        grid=(m // ROW_TILE,),
        in_specs=[
            pl.BlockSpec((ROW_TILE, D_MODEL), row),
            _resident((1, D_MODEL)),
            _resident((D_MODEL, n_out)),
            _resident((1, n_out)),
        ],
        out_specs=[
            pl.BlockSpec((ROW_TILE, D_MODEL), row),
            pl.BlockSpec((ROW_TILE, 2 * SWA_KV_WIDTH), row),
        ],
        out_shape=[
            jax.ShapeDtypeStruct((m, D_MODEL), BF16),
            jax.ShapeDtypeStruct((m, 2 * SWA_KV_WIDTH), BF16),
        ],
        compiler_params=_params("parallel"),
        name="swa_qkv",
    )(x, norm_w, w_qkv, b_qkv)


def _t5_bucket_table():
    qi = np.arange(WINDOW)[:, None]
    kj = np.arange(2 * WINDOW)[None, :]
    dist = qi + WINDOW - kj
    n = np.maximum(dist, 0)
    max_exact = NUM_BUCKETS // 2
    ratio = np.log(np.maximum(n, 1).astype(np.float32) / np.float32(max_exact)) / np.float32(
        math.log(MAX_DISTANCE / max_exact))
    large = max_exact + (ratio * np.float32(NUM_BUCKETS - max_exact)).astype(np.int32)
    large = np.minimum(large, NUM_BUCKETS - 1)
    bucket = np.where(n < max_exact, n, large)
    valid = (dist >= 0) & (dist < WINDOW)
    return np.where(valid, bucket, -1).astype(np.int32)


def _t5_bias_kernel(bucket_ref, rel_ref, o_ref):
    head = pl.program_id(0)
    bucket = bucket_ref[...]
    acc = jnp.full(bucket.shape, -jnp.inf, F32)
    for b in range(NUM_BUCKETS):
        acc = jnp.where(bucket == b, rel_ref[b, head] * LOG2_E, acc)
    o_ref[0, 0] = acc
    col = lax.broadcasted_iota(jnp.int32, bucket.shape, 1)
    o_ref[1, 0] = jnp.where(col < WINDOW, -jnp.inf, acc)


def _t5_bias(rel_bias):
    shape = (WINDOW, 2 * WINDOW)
    return pl.pallas_call(
        _t5_bias_kernel,
        grid=(SWA_HEADS,),
        in_specs=[
            pl.BlockSpec(shape, lambda h: (0, 0)),
            pl.BlockSpec(memory_space=pltpu.SMEM),
        ],
        out_specs=pl.BlockSpec((2, 1) + shape, lambda h: (0, h, 0, 0)),
        out_shape=jax.ShapeDtypeStruct((2, SWA_HEADS) + shape, F32),
        compiler_params=_params("parallel"),
        name="t5_bias",
    )(jnp.asarray(_t5_bucket_table()), rel_bias)


def _swa_attn_kernel(q_ref, kvc_ref, kvp_ref, bias_ref, sink_ref, o_ref):
    hd = SWA_HEAD_DIM
    first_table = jnp.where(pl.program_id(1) == 0, 1, 0)
    zeros = jnp.zeros((2 * WINDOW, hd), BF16)

    for blk in range(ATTN_BLOCKS):
        rows = slice(blk * WINDOW, (blk + 1) * WINDOW)
        if blk == 0:
            kv = jnp.concatenate([kvp_ref[...], kvc_ref[:WINDOW, :]], axis=0)
            table = first_table
        else:
            kv = kvc_ref[(blk - 1) * WINDOW:(blk + 1) * WINDOW, :]
            table = 0
        for g in range(SWA_KV_HEADS):
            k_g = kv[:, g * hd:(g + 1) * hd]
            v_g = kv[:, SWA_KV_WIDTH + g * hd:SWA_KV_WIDTH + (g + 1) * hd]
            k_pad = (jnp.concatenate([k_g, zeros], axis=1), jnp.concatenate([zeros, k_g], axis=1))
            v_pad = (jnp.concatenate([v_g, zeros], axis=1), jnp.concatenate([zeros, v_g], axis=1))
            for j in range(SWA_GROUP // 2):
                pair = g * (SWA_GROUP // 2) + j
                lanes = slice(pair * 2 * hd, (pair + 1) * 2 * hd)
                q_pair = q_ref[rows, lanes]
                out = None
                for e in range(2):
                    head = 2 * pair + e
                    s = _dot_bt(q_pair, k_pad[e]) + bias_ref[table, head]
                    sink = sink_ref[head] * LOG2_E
                    m = jnp.maximum(jnp.max(s, axis=-1, keepdims=True), sink)
                    p = jnp.exp2(s - m)
                    denom = jnp.sum(p, axis=-1, keepdims=True) + jnp.exp2(sink - m)
                    o = _dot(p.astype(BF16), v_pad[e]) / denom
                    out = o if out is None else out + o
                o_ref[rows, lanes] = out.astype(BF16)


def _swa_attn(q, kv, bias, sinks):
    b, s, _ = q.shape
    t = ATTN_BLOCKS * WINDOW
    return pl.pallas_call(
        _swa_attn_kernel,
        grid=(b, s // t),
        in_specs=[
            pl.BlockSpec((None, t, D_MODEL), lambda i, n: (i, n, 0)),
            pl.BlockSpec((None, t, 2 * SWA_KV_WIDTH), lambda i, n: (i, n, 0)),
            pl.BlockSpec((None, WINDOW, 2 * SWA_KV_WIDTH),
                         lambda i, n: (i, jnp.maximum(n * ATTN_BLOCKS - 1, 0), 0)),
            _resident((2, SWA_HEADS, WINDOW, 2 * WINDOW)),
            pl.BlockSpec(memory_space=pltpu.SMEM),
        ],
        out_specs=pl.BlockSpec((None, t, D_MODEL), lambda i, n: (i, n, 0)),
        out_shape=jax.ShapeDtypeStruct((b, s, D_MODEL), BF16),
        compiler_params=_params("parallel", "parallel"),
        name="swa_attn",
    )(q, kv, kv, bias, sinks)


def _dn_in_kernel(x_ref, nw_ref, w_ref, cw_ref, alog_ref, dt_ref, qkv_ref, z_ref, bg_ref, bgt_ref,
                  halo_ref, pbuf_ref):
    tm = ROW_TILE
    gate_cols = 2 * DN_HEADS

    @pl.when(pl.program_id(1) == 0)
    def _():
        halo_ref[...] = jnp.zeros_like(halo_ref)

    h = _rms_norm(x_ref[...], nw_ref[...]).astype(BF16)

    def gate_projection():
        pb = _dot(h, w_ref[:, 4 * DN_WIDTH:])
        beta = _sigmoid(pb)
        t = pb + dt_ref[...]
        softplus = jnp.maximum(t, 0.0) + jnp.log(1.0 + jnp.exp(-jnp.abs(t)))
        g = -jnp.exp(alog_ref[...]) * softplus
        ri = lax.broadcasted_iota(jnp.int32, (DELTA_CHUNK, DELTA_CHUNK), 0)
        ci = lax.broadcasted_iota(jnp.int32, (DELTA_CHUNK, DELTA_CHUNK), 1)
        tri = (ri >= ci).astype(F32)
        col = lax.broadcasted_iota(jnp.int32, (DELTA_CHUNK, gate_cols), 1)
        pad = jnp.zeros((DELTA_CHUNK, LANES - gate_cols), F32)
        for c in range(tm // DELTA_CHUNK):
            rows = slice(c * DELTA_CHUNK, (c + 1) * DELTA_CHUNK)
            gc = jnp.dot(tri, g[rows], preferred_element_type=F32, precision=lax.Precision.HIGHEST)
            gates = jnp.where(col < DN_HEADS, beta[rows], gc)
            bg_ref[rows, :] = gates
            bgt_ref[:, rows] = jnp.concatenate([gates, pad], axis=1).T[:gate_cols, :]

    n_chunks = 3 * DN_WIDTH // CONV_COLS
    z_cols = DN_WIDTH // (n_chunks - 2)
    for c in range(n_chunks):
        c0 = c * CONV_COLS
        cols = slice(c0, c0 + CONV_COLS)
        p = _dot(h, w_ref[:, cols])
        if c < n_chunks - 2:
            z_ref[:, c * z_cols:(c + 1) * z_cols] = _dot(
                h, w_ref[:, 3 * DN_WIDTH + c * z_cols:3 * DN_WIDTH + (c + 1) * z_cols]).astype(BF16)
        elif c == n_chunks - 2:
            gate_projection()
        pbuf_ref[0:SUBLANES, :] = halo_ref[:, cols]
        pbuf_ref[SUBLANES:SUBLANES + tm, :] = p
        halo_ref[:, cols] = p[tm - SUBLANES:, :]
        y = cw_ref[DN_CONV - 1:DN_CONV, cols] * p
        for j in range(DN_CONV - 1):
            start = SUBLANES - (DN_CONV - 1) + j
            y = y + cw_ref[j:j + 1, cols] * pbuf_ref[start:start + tm, :]
        y = y * _sigmoid(y)
        if c0 < 2 * DN_WIDTH:
            scale = DN_HEAD_DIM ** -0.5 if c0 < DN_WIDTH else 1.0
            for hh in range(CONV_COLS // DN_HEAD_DIM):
                t = y[:, hh * DN_HEAD_DIM:(hh + 1) * DN_HEAD_DIM]
                r = lax.rsqrt(jnp.sum(t * t, axis=-1, keepdims=True) + RMS_EPS) * scale
                qkv_ref[:, c0 + hh * DN_HEAD_DIM:c0 + (hh + 1) * DN_HEAD_DIM] = (t * r).astype(BF16)
        else:
            qkv_ref[:, cols] = y.astype(BF16)


def _dn_in(x, norm_w, w_in, conv_w, a_log_row, dt_row):
    b, s, _ = x.shape
    n_in = w_in.shape[1]
    gate_cols = 2 * DN_HEADS
    tok = lambda i, n: (i, n, 0)
    return pl.pallas_call(
        _dn_in_kernel,
        grid=(b, s // ROW_TILE),
        in_specs=[
            pl.BlockSpec((None, ROW_TILE, D_MODEL), tok),
            _resident((1, D_MODEL)),
            _resident((D_MODEL, n_in)),
            _resident((DN_CONV, 3 * DN_WIDTH)),
            _resident((1, gate_cols)),
            _resident((1, gate_cols)),
        ],
        out_specs=[
            pl.BlockSpec((None, ROW_TILE, 3 * DN_WIDTH), tok),
            pl.BlockSpec((None, ROW_TILE, DN_WIDTH), tok),
            pl.BlockSpec((None, ROW_TILE, gate_cols), tok),
            pl.BlockSpec((None, gate_cols, ROW_TILE), lambda i, n: (i, 0, n)),
        ],
        out_shape=[
            jax.ShapeDtypeStruct((b, s, 3 * DN_WIDTH), BF16),
            jax.ShapeDtypeStruct((b, s, DN_WIDTH), BF16),
            jax.ShapeDtypeStruct((b, s, gate_cols), F32),
            jax.ShapeDtypeStruct((b, gate_cols, s), F32),
        ],
        scratch_shapes=[
            pltpu.VMEM((SUBLANES, 3 * DN_WIDTH), F32),
            pltpu.VMEM((SUBLANES + ROW_TILE, CONV_COLS), F32),
        ],
        compiler_params=_params("arbitrary", "arbitrary"),
        name="dn_in",
    )(x, norm_w, w_in, conv_w, a_log_row, dt_row)


def _delta_kernel(qkv_ref, z_ref, bg_ref, bgt_ref, nw_ref, o_ref, state_ref):
    c_len, d = DELTA_CHUNK, DN_HEAD_DIM
    batch = qkv_ref.shape[0]

    @pl.when(pl.program_id(0) == 0)
    def _():
        state_ref[...] = jnp.zeros_like(state_ref)

    ri = lax.broadcasted_iota(jnp.int32, (c_len, c_len), 0)
    ci = lax.broadcasted_iota(jnp.int32, (c_len, c_len), 1)
    causal = ri >= ci
    strict = ri > ci
    eye = (ri == ci).astype(F32)
    sub_blocks = [
        ((ri >> (k + 1)) == (ci >> (k + 1))) & (((ri >> k) & 1) == 1) & (((ci >> k) & 1) == 0)
        for k in range(int(math.log2(c_len)))
    ]
    norm_w = nw_ref[...]

    def operands(i, h):
        q = qkv_ref[i, :, h * d:(h + 1) * d]
        k = qkv_ref[i, :, DN_WIDTH + h * d:DN_WIDTH + (h + 1) * d]
        v = qkv_ref[i, :, 2 * DN_WIDTH + h * d:2 * DN_WIDTH + (h + 1) * d]
        return q, k, v

    def gates(i, h):
        beta_col = bg_ref[i, :, h:h + 1]
        gc_col = bg_ref[i, :, DN_HEADS + h:DN_HEADS + h + 1]
        beta_row = bgt_ref[i, h:h + 1, :]
        gc_row = bgt_ref[i, DN_HEADS + h:DN_HEADS + h + 1, :]
        return beta_col, gc_col, beta_row, gc_row

    problems = [(i, h) for i in range(batch) for h in range(DN_HEADS)]

    attn, neg_lower, inv = {}, {}, {}
    for p in problems:
        q, k, _ = operands(*p)
        beta_col, gc_col, _, gc_row = gates(*p)
        decay = jnp.exp(jnp.where(causal, gc_col - gc_row, -jnp.inf))
        qk = _dot_bt(jnp.concatenate([q, k], axis=0), k)
        attn[p] = (qk[:c_len] * decay).astype(BF16)
        lower = jnp.where(strict, -(qk[c_len:] * (decay * beta_col)), 0.0)
        neg_lower[p] = lower.astype(BF16)
        inv[p] = jnp.where(sub_blocks[0], lower, eye)

    for mask in sub_blocks[1:]:
        inv_b = {p: inv[p].astype(BF16) for p in problems}
        n_t = {p: jnp.where(mask, _dot(neg_lower[p], inv_b[p]), 0.0).astype(BF16) for p in problems}
        inv = {p: inv[p] + _dot(inv_b[p], n_t[p]) for p in problems}

    u, w = {}, {}
    for p in problems:
        _, k, v = operands(*p)
        _, _, beta_row, gc_row = gates(*p)
        u[p] = _dot((inv[p] * beta_row).astype(BF16), v)
        w[p] = _dot((inv[p] * (beta_row * jnp.exp(gc_row))).astype(BF16), k).astype(BF16)

    ws_qs = {}
    for p in problems:
        i, h = p
        q, _, _ = operands(*p)
        ws_qs[p] = _dot(jnp.concatenate([w[p], q], axis=0), state_ref[i * DN_HEADS + h].astype(BF16))

    o_raw, kv = {}, {}
    for p in problems:
        _, k, _ = operands(*p)
        _, gc_col, _, gc_row = gates(*p)
        g_last = gc_row[:, c_len - 1:]
        v_new = u[p] - ws_qs[p][:c_len]
        o_raw[p] = ws_qs[p][c_len:] * jnp.exp(gc_col) + _dot(attn[p], v_new.astype(BF16))
        kv[p] = _dot_at(k, (v_new * jnp.exp(g_last - gc_col)).astype(BF16))

    for p in problems:
        i, h = p
        lanes = slice(h * d, (h + 1) * d)
        g_last = bgt_ref[i, DN_HEADS + h:DN_HEADS + h + 1, c_len - 1:]
        state_ref[i * DN_HEADS + h] = state_ref[i * DN_HEADS + h] * jnp.exp(g_last) + kv[p]
        zf = z_ref[i, :, lanes].astype(F32)
        o = o_raw[p]
        o = o * lax.rsqrt(jnp.mean(o * o, axis=-1, keepdims=True) + RMS_EPS) * norm_w * (zf * _sigmoid(zf))
        o_ref[i, :, lanes] = o.astype(BF16)


def _delta_rule(qkv, z, bg, bgt, norm_w):
    b, s, _ = qkv.shape
    t = DELTA_CHUNK
    tok = lambda n: (0, n, 0)
    return pl.pallas_call(
        _delta_kernel,
        grid=(s // t,),
        in_specs=[
            pl.BlockSpec((b, t, 3 * DN_WIDTH), tok),
            pl.BlockSpec((b, t, DN_WIDTH), tok),
            pl.BlockSpec((b, t, 2 * DN_HEADS), tok),
            pl.BlockSpec((b, 2 * DN_HEADS, t), lambda n: (0, 0, n)),
            _resident((1, DN_HEAD_DIM)),
        ],
        out_specs=pl.BlockSpec((b, t, DN_WIDTH), tok),
        out_shape=jax.ShapeDtypeStruct((b, s, DN_WIDTH), BF16),
        scratch_shapes=[pltpu.VMEM((b * DN_HEADS, DN_HEAD_DIM, DN_HEAD_DIM), F32)],
        compiler_params=_params("arbitrary"),
        name="delta_rule",
    )(qkv, z, bg, bgt, norm_w)


def _row(v):
    return v.reshape(1, -1).astype(F32)


def _gate_row(v):
    return jnp.concatenate([jnp.zeros_like(v), v]).reshape(1, -1).astype(F32)


def kernel(x, norm_mix, norm_mlp, norm_final, dn_w_in, dn_conv_w, dn_a_log, dn_dt_bias, dn_norm_w, dn_w_out,
           swa_w_qkv, swa_b_qkv, swa_sinks, swa_w_out, swa_b_out, rel_bias, mlp_w_up, mlp_w_down):
    b, s, d = x.shape
    depth = norm_mix.shape[0]
    m = b * s
    bias = _t5_bias(rel_bias.astype(F32))
    zero_bias = jnp.zeros((1, d), F32)
    final_w = _row(norm_final)
    dn_w_in_bf16 = dn_w_in.astype(BF16)
    for i in range(depth):
        j = i // 2
        if i % 2 == 0:
            qkv, z, bg, bgt = _dn_in(x, _row(norm_mix[i]), dn_w_in_bf16[j], dn_conv_w[j].astype(F32),
                                     _gate_row(dn_a_log[j]), _gate_row(dn_dt_bias[j]))
            mixed = _delta_rule(qkv, z, bg, bgt, _row(dn_norm_w[j]))
            w_out, b_out = dn_w_out[j], zero_bias
        else:
            q, kv = _swa_qkv(x.reshape(m, d), _row(norm_mix[i]), swa_w_qkv[j].astype(BF16), _row(swa_b_qkv[j]))
            mixed = _swa_attn(q.reshape(b, s, d), kv.reshape(b, s, 2 * SWA_KV_WIDTH), bias,
                              swa_sinks[j].astype(F32))
            w_out, b_out = swa_w_out[j], _row(swa_b_out[j])
        x = _out_mlp(x.reshape(m, d), mixed.reshape(m, d), w_out.astype(BF16), b_out, _row(norm_mlp[i]),
                     mlp_w_up[i].astype(BF16), mlp_w_down[i].astype(BF16), final_w,
                     final_norm=(i == depth - 1)).reshape(b, s, d)
    return x
```

```python
import functools
import math

import numpy as np
import jax
import jax.numpy as jnp
from jax import lax
from jax.experimental import pallas as pl
from jax.experimental.pallas import tpu as pltpu

F32 = jnp.float32
BF16 = jnp.bfloat16

D_MODEL = 1024
RMS_EPS = 1e-6
DN_HEAD_DIM = 128
DN_HEADS = 8
DN_WIDTH = DN_HEADS * DN_HEAD_DIM
DN_CONV = 4
SWA_HEAD_DIM = 64
SWA_HEADS = 16
SWA_KV_HEADS = 2
SWA_GROUP = SWA_HEADS // SWA_KV_HEADS
SWA_KV_WIDTH = SWA_KV_HEADS * SWA_HEAD_DIM
WINDOW = 128
NUM_BUCKETS = 32
MAX_DISTANCE = 128
LOG2_E = math.log2(math.e)
D_FF = 4 * D_MODEL

V7X_VMEM_BYTES = 64 * 1024 * 1024
VMEM_LIMIT_BYTES = V7X_VMEM_BYTES * 7 // 8
SUBLANES = 8
LANES = 128

ROW_TILE = 512
MLP_TILE = 1024
QKV_TILE = 1024
FF_CHUNK = 1024
CONV_COLS = 512
ATTN_BLOCKS = 4
DELTA_CHUNK = 128


def _resident(shape):
    return pl.BlockSpec(shape, lambda *_: (0,) * len(shape), pipeline_mode=pl.Buffered(1))


def _params(*semantics):
    return pltpu.CompilerParams(dimension_semantics=semantics, vmem_limit_bytes=VMEM_LIMIT_BYTES)


def _rms_norm(x, w):
    return x * lax.rsqrt(jnp.mean(x * x, axis=-1, keepdims=True) + RMS_EPS) * w


def _sigmoid(x):
    return 1.0 / (1.0 + jnp.exp(-x))


def _dot(a, b):
    return jnp.dot(a, b, preferred_element_type=F32)


def _dot_bt(a, b):
    return lax.dot_general(a, b, (((1,), (1,)), ((), ())), preferred_element_type=F32)


def _dot_at(a, b):
    return lax.dot_general(a, b, (((0,), (0,)), ((), ())), preferred_element_type=F32)


def _out_mlp_kernel(x_ref, a_ref, wo_ref, bo_ref, nw_ref, wup_ref, wdn_ref, fw_ref, o_ref, *, final_norm):
    x1 = x_ref[...] + _dot(a_ref[...], wo_ref[...]) + bo_ref[...]
    h = _rms_norm(x1, nw_ref[...]).astype(BF16)
    acc = x1
    for c in range(D_FF // FF_CHUNK):
        cols = slice(c * FF_CHUNK, (c + 1) * FF_CHUNK)
        u = jnp.square(jnp.maximum(_dot(h, wup_ref[:, cols]), 0.0)).astype(BF16)
        acc = acc + _dot(u, wdn_ref[cols, :])
    if final_norm:
        acc = _rms_norm(acc, fw_ref[...])
    o_ref[...] = acc


def _out_mlp(x, a, w_out, b_out, norm_w, w_up, w_down, final_w, final_norm):
    m = x.shape[0]
    row = lambda i: (i, 0)
    return pl.pallas_call(
        functools.partial(_out_mlp_kernel, final_norm=final_norm),
        grid=(m // MLP_TILE,),
        in_specs=[
            pl.BlockSpec((MLP_TILE, D_MODEL), row),
            pl.BlockSpec((MLP_TILE, D_MODEL), row),
            _resident((D_MODEL, D_MODEL)),
            _resident((1, D_MODEL)),
            _resident((1, D_MODEL)),
            _resident((D_MODEL, D_FF)),
            _resident((D_FF, D_MODEL)),
            _resident((1, D_MODEL)),
        ],
        out_specs=pl.BlockSpec((MLP_TILE, D_MODEL), row),
        out_shape=jax.ShapeDtypeStruct((m, D_MODEL), F32),
        compiler_params=_params("parallel"),
        name="out_mlp",
    )(x, a, w_out, b_out, norm_w, w_up, w_down, final_w)


def _swa_qkv_kernel(x_ref, nw_ref, w_ref, b_ref, q_ref, kv_ref):
    h = _rms_norm(x_ref[...], nw_ref[...]).astype(BF16)
    y = _dot(h, w_ref[...]) + b_ref[...]
    q_ref[...] = (y[:, :D_MODEL] * (SWA_HEAD_DIM ** -0.5 * LOG2_E)).astype(BF16)
    kv_ref[...] = y[:, D_MODEL:].astype(BF16)


def _swa_qkv(x, norm_w, w_qkv, b_qkv):
    m = x.shape[0]
    n_out = D_MODEL + 2 * SWA_KV_WIDTH
    row = lambda i: (i, 0)
    return pl.pallas_call(
        _swa_qkv_kernel,
        grid=(m // QKV_TILE,),
        in_specs=[
            pl.BlockSpec((QKV_TILE, D_MODEL), row),
            _resident((1, D_MODEL)),
            _resident((D_MODEL, n_out)),
            _resident((1, n_out)),
        ],
        out_specs=[
            pl.BlockSpec((QKV_TILE, D_MODEL), row),
            pl.BlockSpec((QKV_TILE, 2 * SWA_KV_WIDTH), row),
        ],
        out_shape=[
            jax.ShapeDtypeStruct((m, D_MODEL), BF16),
            jax.ShapeDtypeStruct((m, 2 * SWA_KV_WIDTH), BF16),
        ],
        compiler_params=_params("parallel"),
        name="swa_qkv",
    )(x, norm_w, w_qkv, b_qkv)


def _t5_bucket_table():
    qi = np.arange(WINDOW)[:, None]
    kj = np.arange(2 * WINDOW)[None, :]
    dist = qi + WINDOW - kj
    n = np.maximum(dist, 0)
    max_exact = NUM_BUCKETS // 2
    ratio = np.log(np.maximum(n, 1).astype(np.float32) / np.float32(max_exact)) / np.float32(
        math.log(MAX_DISTANCE / max_exact))
    large = max_exact + (ratio * np.float32(NUM_BUCKETS - max_exact)).astype(np.int32)
    large = np.minimum(large, NUM_BUCKETS - 1)
    bucket = np.where(n < max_exact, n, large)
    valid = (dist >= 0) & (dist < WINDOW)
    return np.where(valid, bucket, -1).astype(np.int32)


def _t5_bias_kernel(bucket_ref, rel_ref, o_ref):
    head = pl.program_id(0)
    bucket = bucket_ref[...]
    acc = jnp.full(bucket.shape, -jnp.inf, F32)
    for b in range(NUM_BUCKETS):
        acc = jnp.where(bucket == b, rel_ref[b, head] * LOG2_E, acc)
    o_ref[0, 0] = acc
    col = lax.broadcasted_iota(jnp.int32, bucket.shape, 1)
    o_ref[1, 0] = jnp.where(col < WINDOW, -jnp.inf, acc)


def _t5_bias(rel_bias):
    shape = (WINDOW, 2 * WINDOW)
    return pl.pallas_call(
        _t5_bias_kernel,
        grid=(SWA_HEADS,),
        in_specs=[
            pl.BlockSpec(shape, lambda h: (0, 0)),
            pl.BlockSpec(memory_space=pltpu.SMEM),
        ],
        out_specs=pl.BlockSpec((2, 1) + shape, lambda h: (0, h, 0, 0)),
        out_shape=jax.ShapeDtypeStruct((2, SWA_HEADS) + shape, F32),
        compiler_params=_params("parallel"),
        name="t5_bias",
    )(jnp.asarray(_t5_bucket_table()), rel_bias)


def _swa_attn_kernel(q_ref, kvc_ref, kvp_ref, bias_ref, sink_ref, o_ref):
    hd = SWA_HEAD_DIM
    first_table = jnp.where(pl.program_id(1) == 0, 1, 0)
    zeros = jnp.zeros((2 * WINDOW, hd), BF16)

    for blk in range(ATTN_BLOCKS):
        rows = slice(blk * WINDOW, (blk + 1) * WINDOW)
        if blk == 0:
            kv = jnp.concatenate([kvp_ref[...], kvc_ref[:WINDOW, :]], axis=0)
            table = first_table
        else:
            kv = kvc_ref[(blk - 1) * WINDOW:(blk + 1) * WINDOW, :]
            table = 0
        for g in range(SWA_KV_HEADS):
            k_g = kv[:, g * hd:(g + 1) * hd]
            v_g = kv[:, SWA_KV_WIDTH + g * hd:SWA_KV_WIDTH + (g + 1) * hd]
            k_pad = (jnp.concatenate([k_g, zeros], axis=1), jnp.concatenate([zeros, k_g], axis=1))
            v_pad = (jnp.concatenate([v_g, zeros], axis=1), jnp.concatenate([zeros, v_g], axis=1))
            for j in range(SWA_GROUP // 2):
                pair = g * (SWA_GROUP // 2) + j
                lanes = slice(pair * 2 * hd, (pair + 1) * 2 * hd)
                q_pair = q_ref[rows, lanes]
                out = None
                for e in range(2):
                    head = 2 * pair + e
                    s = _dot_bt(q_pair, k_pad[e]) + bias_ref[table, head]
                    sink = sink_ref[head] * LOG2_E
                    m = jnp.maximum(jnp.max(s, axis=-1, keepdims=True), sink)
                    p = jnp.exp2(s - m)
                    denom = jnp.sum(p, axis=-1, keepdims=True) + jnp.exp2(sink - m)
                    o = _dot(p.astype(BF16), v_pad[e]) / denom
                    out = o if out is None else out + o
                o_ref[rows, lanes] = out.astype(BF16)


def _swa_attn(q, kv, bias, sinks):
    b, s, _ = q.shape
    t = ATTN_BLOCKS * WINDOW
    return pl.pallas_call(
        _swa_attn_kernel,
        grid=(b, s // t),
        in_specs=[
            pl.BlockSpec((None, t, D_MODEL), lambda i, n: (i, n, 0)),
            pl.BlockSpec((None, t, 2 * SWA_KV_WIDTH), lambda i, n: (i, n, 0)),
            pl.BlockSpec((None, WINDOW, 2 * SWA_KV_WIDTH),
                         lambda i, n: (i, jnp.maximum(n * ATTN_BLOCKS - 1, 0), 0)),
            _resident((2, SWA_HEADS, WINDOW, 2 * WINDOW)),
            pl.BlockSpec(memory_space=pltpu.SMEM),
        ],
        out_specs=pl.BlockSpec((None, t, D_MODEL), lambda i, n: (i, n, 0)),
        out_shape=jax.ShapeDtypeStruct((b, s, D_MODEL), BF16),
        compiler_params=_params("parallel", "parallel"),
        name="swa_attn",
    )(q, kv, kv, bias, sinks)


def _dn_in_kernel(x_ref, nw_ref, w_ref, cw_ref, alog_ref, dt_ref, qkv_ref, z_ref, bg_ref, bgt_ref,
                  halo_ref, pbuf_ref):
    tm = ROW_TILE
    gate_cols = 2 * DN_HEADS

    @pl.when(pl.program_id(1) == 0)
    def _():
        halo_ref[...] = jnp.zeros_like(halo_ref)

    h = _rms_norm(x_ref[...], nw_ref[...]).astype(BF16)

    def gate_projection():
        pb = _dot(h, w_ref[:, 4 * DN_WIDTH:])
        beta = _sigmoid(pb)
        t = pb + dt_ref[...]
        softplus = jnp.maximum(t, 0.0) + jnp.log(1.0 + jnp.exp(-jnp.abs(t)))
        g = -jnp.exp(alog_ref[...]) * softplus
        ri = lax.broadcasted_iota(jnp.int32, (DELTA_CHUNK, DELTA_CHUNK), 0)
        ci = lax.broadcasted_iota(jnp.int32, (DELTA_CHUNK, DELTA_CHUNK), 1)
        tri = (ri >= ci).astype(F32)
        col = lax.broadcasted_iota(jnp.int32, (DELTA_CHUNK, gate_cols), 1)
        pad = jnp.zeros((DELTA_CHUNK, LANES - gate_cols), F32)
        for c in range(tm // DELTA_CHUNK):
            rows = slice(c * DELTA_CHUNK, (c + 1) * DELTA_CHUNK)
            gc = jnp.dot(tri, g[rows], preferred_element_type=F32, precision=lax.Precision.HIGHEST)
            gates = jnp.where(col < DN_HEADS, beta[rows], gc)
            bg_ref[rows, :] = gates
            bgt_ref[:, rows] = jnp.concatenate([gates, pad], axis=1).T[:gate_cols, :]

    n_chunks = 3 * DN_WIDTH // CONV_COLS
    z_cols = DN_WIDTH // (n_chunks - 2)
    for c in range(n_chunks):
        c0 = c * CONV_COLS
        cols = slice(c0, c0 + CONV_COLS)
        p = _dot(h, w_ref[:, cols])
        if c < n_chunks - 2:
            z_ref[:, c * z_cols:(c + 1) * z_cols] = _dot(
                h, w_ref[:, 3 * DN_WIDTH + c * z_cols:3 * DN_WIDTH + (c + 1) * z_cols]).astype(BF16)
        elif c == n_chunks - 2:
            gate_projection()
        pbuf_ref[0:SUBLANES, :] = halo_ref[:, cols]
        pbuf_ref[SUBLANES:SUBLANES + tm, :] = p
        halo_ref[:, cols] = p[tm - SUBLANES:, :]
        y = cw_ref[DN_CONV - 1:DN_CONV, cols] * p
        for j in range(DN_CONV - 1):
            start = SUBLANES - (DN_CONV - 1) + j
            y = y + cw_ref[j:j + 1, cols] * pbuf_ref[start:start + tm, :]
        y = y * _sigmoid(y)
        if c0 < 2 * DN_WIDTH:
            scale = DN_HEAD_DIM ** -0.5 if c0 < DN_WIDTH else 1.0
            for hh in range(CONV_COLS // DN_HEAD_DIM):
                t = y[:, hh * DN_HEAD_DIM:(hh + 1) * DN_HEAD_DIM]
                r = lax.rsqrt(jnp.sum(t * t, axis=-1, keepdims=True) + RMS_EPS) * scale
                qkv_ref[:, c0 + hh * DN_HEAD_DIM:c0 + (hh + 1) * DN_HEAD_DIM] = (t * r).astype(BF16)
        else:
            qkv_ref[:, cols] = y.astype(BF16)


def _dn_in(x, norm_w, w_in, conv_w, a_log_row, dt_row):
    b, s, _ = x.shape
    n_in = w_in.shape[1]
    gate_cols = 2 * DN_HEADS
    tok = lambda i, n: (i, n, 0)
    return pl.pallas_call(
        _dn_in_kernel,
        grid=(b, s // ROW_TILE),
        in_specs=[
            pl.BlockSpec((None, ROW_TILE, D_MODEL), tok),
            _resident((1, D_MODEL)),
            _resident((D_MODEL, n_in)),
            _resident((DN_CONV, 3 * DN_WIDTH)),
            _resident((1, gate_cols)),
            _resident((1, gate_cols)),
        ],
        out_specs=[
            pl.BlockSpec((None, ROW_TILE, 3 * DN_WIDTH), tok),
            pl.BlockSpec((None, ROW_TILE, DN_WIDTH), tok),
            pl.BlockSpec((None, ROW_TILE, gate_cols), tok),
            pl.BlockSpec((None, gate_cols, ROW_TILE), lambda i, n: (i, 0, n)),
        ],
        out_shape=[
            jax.ShapeDtypeStruct((b, s, 3 * DN_WIDTH), BF16),
            jax.ShapeDtypeStruct((b, s, DN_WIDTH), BF16),
            jax.ShapeDtypeStruct((b, s, gate_cols), F32),
            jax.ShapeDtypeStruct((b, gate_cols, s), F32),
        ],
        scratch_shapes=[
            pltpu.VMEM((SUBLANES, 3 * DN_WIDTH), F32),
            pltpu.VMEM((SUBLANES + ROW_TILE, CONV_COLS), F32),
        ],
        compiler_params=_params("arbitrary", "arbitrary"),
        name="dn_in",
    )(x, norm_w, w_in, conv_w, a_log_row, dt_row)


def _delta_kernel(qkv_ref, z_ref, bg_ref, bgt_ref, nw_ref, o_ref, state_ref):
    c_len, d = DELTA_CHUNK, DN_HEAD_DIM
    batch = qkv_ref.shape[0]

    @pl.when(pl.program_id(0) == 0)
    def _():
        state_ref[...] = jnp.zeros_like(state_ref)

    ri = lax.broadcasted_iota(jnp.int32, (c_len, c_len), 0)
    ci = lax.broadcasted_iota(jnp.int32, (c_len, c_len), 1)
    causal = ri >= ci
    strict = ri > ci
    eye = (ri == ci).astype(F32)
    sub_blocks = [
        ((ri >> (k + 1)) == (ci >> (k + 1))) & (((ri >> k) & 1) == 1) & (((ci >> k) & 1) == 0)
        for k in range(int(math.log2(c_len)))
    ]
    norm_w = nw_ref[...]

    def operands(i, h):
        q = qkv_ref[i, :, h * d:(h + 1) * d]
        k = qkv_ref[i, :, DN_WIDTH + h * d:DN_WIDTH + (h + 1) * d]
        v = qkv_ref[i, :, 2 * DN_WIDTH + h * d:2 * DN_WIDTH + (h + 1) * d]
        return q, k, v

    def gates(i, h):
        beta_col = bg_ref[i, :, h:h + 1]
        gc_col = bg_ref[i, :, DN_HEADS + h:DN_HEADS + h + 1]
        beta_row = bgt_ref[i, h:h + 1, :]
        gc_row = bgt_ref[i, DN_HEADS + h:DN_HEADS + h + 1, :]
        return beta_col, gc_col, beta_row, gc_row

    problems = [(i, h) for i in range(batch) for h in range(DN_HEADS)]

    attn, neg_lower, inv = {}, {}, {}
    for p in problems:
        q, k, _ = operands(*p)
        beta_col, gc_col, _, gc_row = gates(*p)
        decay = jnp.exp(jnp.where(causal, gc_col - gc_row, -jnp.inf))
        qk = _dot_bt(jnp.concatenate([q, k], axis=0), k)
        attn[p] = (qk[:c_len] * decay).astype(BF16)
        lower = jnp.where(strict, -(qk[c_len:] * (decay * beta_col)), 0.0)
        neg_lower[p] = lower.astype(BF16)
        inv[p] = jnp.where(sub_blocks[0], lower, eye)

    for mask in sub_blocks[1:]:
        inv_b = {p: inv[p].astype(BF16) for p in problems}
        n_t = {p: jnp.where(mask, _dot(neg_lower[p], inv_b[p]), 0.0).astype(BF16) for p in problems}
        inv = {p: inv[p] + _dot(inv_b[p], n_t[p]) for p in problems}

    u, w = {}, {}
    for p in problems:
        _, k, v = operands(*p)
        _, _, beta_row, gc_row = gates(*p)
        u[p] = _dot((inv[p] * beta_row).astype(BF16), v)
        w[p] = _dot((inv[p] * (beta_row * jnp.exp(gc_row))).astype(BF16), k).astype(BF16)

    ws_qs = {}
    for p in problems:
        i, h = p
        q, _, _ = operands(*p)
        ws_qs[p] = _dot(jnp.concatenate([w[p], q], axis=0), state_ref[i * DN_HEADS + h].astype(BF16))

    o_raw, kv = {}, {}
    for p in problems:
        _, k, _ = operands(*p)
        _, gc_col, _, gc_row = gates(*p)
        g_last = gc_row[:, c_len - 1:]
        v_new = u[p] - ws_qs[p][:c_len]
        o_raw[p] = ws_qs[p][c_len:] * jnp.exp(gc_col) + _dot(attn[p], v_new.astype(BF16))
        kv[p] = _dot_at(k, (v_new * jnp.exp(g_last - gc_col)).astype(BF16))

    for p in problems:
        i, h = p
        lanes = slice(h * d, (h + 1) * d)
        g_last = bgt_ref[i, DN_HEADS + h:DN_HEADS + h + 1, c_len - 1:]
        state_ref[i * DN_HEADS + h] = state_ref[i * DN_HEADS + h] * jnp.exp(g_last) + kv[p]
        zf = z_ref[i, :, lanes].astype(F32)
        o = o_raw[p]
        o = o * lax.rsqrt(jnp.mean(o * o, axis=-1, keepdims=True) + RMS_EPS) * norm_w * (zf * _sigmoid(zf))
        o_ref[i, :, lanes] = o.astype(BF16)


def _delta_rule(qkv, z, bg, bgt, norm_w):
    b, s, _ = qkv.shape
    t = DELTA_CHUNK
    tok = lambda n: (0, n, 0)
    return pl.pallas_call(
        _delta_kernel,
        grid=(s // t,),
        in_specs=[
            pl.BlockSpec((b, t, 3 * DN_WIDTH), tok),
            pl.BlockSpec((b, t, DN_WIDTH), tok),
            pl.BlockSpec((b, t, 2 * DN_HEADS), tok),
            pl.BlockSpec((b, 2 * DN_HEADS, t), lambda n: (0, 0, n)),
            _resident((1, DN_HEAD_DIM)),
        ],
        out_specs=pl.BlockSpec((b, t, DN_WIDTH), tok),
        out_shape=jax.ShapeDtypeStruct((b, s, DN_WIDTH), BF16),
        scratch_shapes=[pltpu.VMEM((b * DN_HEADS, DN_HEAD_DIM, DN_HEAD_DIM), F32)],
        compiler_params=_params("arbitrary"),
        name="delta_rule",
    )(qkv, z, bg, bgt, norm_w)


def _row(v):
    return v.reshape(1, -1).astype(F32)


def _gate_row(v):
    return jnp.concatenate([jnp.zeros_like(v), v]).reshape(1, -1).astype(F32)


def kernel(x, norm_mix, norm_mlp, norm_final, dn_w_in, dn_conv_w, dn_a_log, dn_dt_bias, dn_norm_w, dn_w_out,
           swa_w_qkv, swa_b_qkv, swa_sinks, swa_w_out, swa_b_out, rel_bias, mlp_w_up, mlp_w_down):
    b, s, d = x.shape
    depth = norm_mix.shape[0]
    m = b * s
    bias = _t5_bias(rel_bias.astype(F32))
    zero_bias = jnp.zeros((1, d), F32)
    final_w = _row(norm_final)
    for i in range(depth):
        j = i // 2
        if i % 2 == 0:
            qkv, z, bg, bgt = _dn_in(x, _row(norm_mix[i]), dn_w_in[j].astype(BF16), dn_conv_w[j].astype(F32),
                                     _gate_row(dn_a_log[j]), _gate_row(dn_dt_bias[j]))
            mixed = _delta_rule(qkv, z, bg, bgt, _row(dn_norm_w[j]))
            w_out, b_out = dn_w_out[j], zero_bias
        else:
            q, kv = _swa_qkv(x.reshape(m, d), _row(norm_mix[i]), swa_w_qkv[j].astype(BF16), _row(swa_b_qkv[j]))
            mixed = _swa_attn(q.reshape(b, s, d), kv.reshape(b, s, 2 * SWA_KV_WIDTH), bias,
                              swa_sinks[j].astype(F32))
            w_out, b_out = swa_w_out[j], _row(swa_b_out[j])
        x = _out_mlp(x.reshape(m, d), mixed.reshape(m, d), w_out.astype(BF16), b_out, _row(norm_mlp[i]),
                     mlp_w_up[i].astype(BF16), mlp_w_down[i].astype(BF16), final_w,
                     final_norm=(i == depth - 1)).reshape(b, s, d)
    return x
```

```python
import functools
import math

import numpy as np
import jax
import jax.numpy as jnp
from jax import lax
from jax.experimental import pallas as pl
from jax.experimental.pallas import tpu as pltpu

F32 = jnp.float32
BF16 = jnp.bfloat16

D_MODEL = 1024
RMS_EPS = 1e-6
DN_HEAD_DIM = 128
DN_HEADS = 8
DN_WIDTH = DN_HEADS * DN_HEAD_DIM
DN_CONV = 4
SWA_HEAD_DIM = 64
SWA_HEADS = 16
SWA_KV_HEADS = 2
SWA_GROUP = SWA_HEADS // SWA_KV_HEADS
SWA_KV_WIDTH = SWA_KV_HEADS * SWA_HEAD_DIM
WINDOW = 128
NUM_BUCKETS = 32
MAX_DISTANCE = 128
LOG2_E = math.log2(math.e)
D_FF = 4 * D_MODEL

V7X_VMEM_BYTES = 64 * 1024 * 1024
VMEM_LIMIT_BYTES = V7X_VMEM_BYTES * 7 // 8
SUBLANES = 8
LANES = 128

ROW_TILE = 512
MLP_TILE = 1024
QKV_TILE = 1024
FF_CHUNK = 1024
CONV_COLS = 512
ATTN_BLOCKS = 4
DELTA_CHUNK = 128


def _resident(shape):
    return pl.BlockSpec(shape, lambda *_: (0,) * len(shape), pipeline_mode=pl.Buffered(1))


def _resident_layer(shape, layer):
    return pl.BlockSpec((None,) + shape, lambda *_: (layer,) + (0,) * len(shape), pipeline_mode=pl.Buffered(1))


def _params(*semantics):
    return pltpu.CompilerParams(dimension_semantics=semantics, vmem_limit_bytes=VMEM_LIMIT_BYTES)


def _rms_norm(x, w):
    return x * lax.rsqrt(jnp.mean(x * x, axis=-1, keepdims=True) + RMS_EPS) * w


def _sigmoid(x):
    return 1.0 / (1.0 + jnp.exp(-x))


def _dot(a, b):
    return jnp.dot(a, b, preferred_element_type=F32)


def _dot_bt(a, b):
    return lax.dot_general(a, b, (((1,), (1,)), ((), ())), preferred_element_type=F32)


def _dot_at(a, b):
    return lax.dot_general(a, b, (((0,), (0,)), ((), ())), preferred_element_type=F32)


def _out_mlp_kernel(x_ref, a_ref, wo_ref, bo_ref, nw_ref, wup_ref, wdn_ref, fw_ref, o_ref, *, final_norm):
    x1 = x_ref[...] + _dot(a_ref[...], wo_ref[...]) + bo_ref[...]
    h = _rms_norm(x1, nw_ref[...]).astype(BF16)
    acc = x1
    for c in range(D_FF // FF_CHUNK):
        cols = slice(c * FF_CHUNK, (c + 1) * FF_CHUNK)
        u = jnp.square(jnp.maximum(_dot(h, wup_ref[:, cols]), 0.0)).astype(BF16)
        acc = acc + _dot(u, wdn_ref[cols, :])
    if final_norm:
        acc = _rms_norm(acc, fw_ref[...])
    o_ref[...] = acc


def _out_mlp(x, a, w_out, mixer_layer, b_out, norm_w, w_up, w_down, layer, final_w, final_norm):
    m = x.shape[0]
    row = lambda i: (i, 0)
    return pl.pallas_call(
        functools.partial(_out_mlp_kernel, final_norm=final_norm),
        grid=(m // MLP_TILE,),
        in_specs=[
            pl.BlockSpec((MLP_TILE, D_MODEL), row),
            pl.BlockSpec((MLP_TILE, D_MODEL), row),
            _resident_layer((D_MODEL, D_MODEL), mixer_layer),
            _resident((1, D_MODEL)),
            _resident((1, D_MODEL)),
            _resident_layer((D_MODEL, D_FF), layer),
            _resident_layer((D_FF, D_MODEL), layer),
            _resident((1, D_MODEL)),
        ],
        out_specs=pl.BlockSpec((MLP_TILE, D_MODEL), row),
        out_shape=jax.ShapeDtypeStruct((m, D_MODEL), F32),
        compiler_params=_params("parallel"),
        name="out_mlp",
    )(x, a, w_out, b_out, norm_w, w_up, w_down, final_w)


def _swa_qkv_kernel(x_ref, nw_ref, w_ref, b_ref, q_ref, kv_ref):
    h = _rms_norm(x_ref[...], nw_ref[...]).astype(BF16)
    y = _dot(h, w_ref[...]) + b_ref[...]
    q_ref[...] = (y[:, :D_MODEL] * (SWA_HEAD_DIM ** -0.5 * LOG2_E)).astype(BF16)
    kv_ref[...] = y[:, D_MODEL:].astype(BF16)


def _swa_qkv(x, norm_w, w_qkv, layer, b_qkv):
    m = x.shape[0]
    n_out = D_MODEL + 2 * SWA_KV_WIDTH
    row = lambda i: (i, 0)
    return pl.pallas_call(
        _swa_qkv_kernel,
        grid=(m // QKV_TILE,),
        in_specs=[
            pl.BlockSpec((QKV_TILE, D_MODEL), row),
            _resident((1, D_MODEL)),
            _resident_layer((D_MODEL, n_out), layer),
            _resident((1, n_out)),
        ],
        out_specs=[
            pl.BlockSpec((QKV_TILE, D_MODEL), row),
            pl.BlockSpec((QKV_TILE, 2 * SWA_KV_WIDTH), row),
        ],
        out_shape=[
            jax.ShapeDtypeStruct((m, D_MODEL), BF16),
            jax.ShapeDtypeStruct((m, 2 * SWA_KV_WIDTH), BF16),
        ],
        compiler_params=_params("parallel"),
        name="swa_qkv",
    )(x, norm_w, w_qkv, b_qkv)


def _t5_bucket_table():
    qi = np.arange(WINDOW)[:, None]
    kj = np.arange(2 * WINDOW)[None, :]
    dist = qi + WINDOW - kj
    n = np.maximum(dist, 0)
    max_exact = NUM_BUCKETS // 2
    ratio = np.log(np.maximum(n, 1).astype(np.float32) / np.float32(max_exact)) / np.float32(
        math.log(MAX_DISTANCE / max_exact))
    large = max_exact + (ratio * np.float32(NUM_BUCKETS - max_exact)).astype(np.int32)
    large = np.minimum(large, NUM_BUCKETS - 1)
    bucket = np.where(n < max_exact, n, large)
    valid = (dist >= 0) & (dist < WINDOW)
    return np.where(valid, bucket, -1).astype(np.int32)


def _t5_bias_kernel(bucket_ref, rel_ref, o_ref):
    head = pl.program_id(0)
    bucket = bucket_ref[...]
    acc = jnp.full(bucket.shape, -jnp.inf, F32)
    for b in range(NUM_BUCKETS):
        acc = jnp.where(bucket == b, rel_ref[b, head] * LOG2_E, acc)
    o_ref[0, 0] = acc
    col = lax.broadcasted_iota(jnp.int32, bucket.shape, 1)
    o_ref[1, 0] = jnp.where(col < WINDOW, -jnp.inf, acc)


def _t5_bias(rel_bias):
    shape = (WINDOW, 2 * WINDOW)
    return pl.pallas_call(
        _t5_bias_kernel,
        grid=(SWA_HEADS,),
        in_specs=[
            pl.BlockSpec(shape, lambda h: (0, 0)),
            pl.BlockSpec(memory_space=pltpu.SMEM),
        ],
        out_specs=pl.BlockSpec((2, 1) + shape, lambda h: (0, h, 0, 0)),
        out_shape=jax.ShapeDtypeStruct((2, SWA_HEADS) + shape, F32),
        compiler_params=_params("parallel"),
        name="t5_bias",
    )(jnp.asarray(_t5_bucket_table()), rel_bias)


def _swa_attn_kernel(q_ref, kvc_ref, kvp_ref, bias_ref, sink_ref, o_ref):
    hd = SWA_HEAD_DIM
    first_table = jnp.where(pl.program_id(1) == 0, 1, 0)
    zeros = jnp.zeros((2 * WINDOW, hd), BF16)

    for blk in range(ATTN_BLOCKS):
        rows = slice(blk * WINDOW, (blk + 1) * WINDOW)
        if blk == 0:
            kv = jnp.concatenate([kvp_ref[...], kvc_ref[:WINDOW, :]], axis=0)
            table = first_table
        else:
            kv = kvc_ref[(blk - 1) * WINDOW:(blk + 1) * WINDOW, :]
            table = 0
        for g in range(SWA_KV_HEADS):
            k_g = kv[:, g * hd:(g + 1) * hd]
            v_g = kv[:, SWA_KV_WIDTH + g * hd:SWA_KV_WIDTH + (g + 1) * hd]
            k_pad = (jnp.concatenate([k_g, zeros], axis=1), jnp.concatenate([zeros, k_g], axis=1))
            v_pad = (jnp.concatenate([v_g, zeros], axis=1), jnp.concatenate([zeros, v_g], axis=1))
            for j in range(SWA_GROUP // 2):
                pair = g * (SWA_GROUP // 2) + j
                lanes = slice(pair * 2 * hd, (pair + 1) * 2 * hd)
                q_pair = q_ref[rows, lanes]
                out = None
                for e in range(2):
                    head = 2 * pair + e
                    s = _dot_bt(q_pair, k_pad[e]) + bias_ref[table, head]
                    sink = sink_ref[head] * LOG2_E
                    m = jnp.maximum(jnp.max(s, axis=-1, keepdims=True), sink)
                    p = jnp.exp2(s - m)
                    denom = jnp.sum(p, axis=-1, keepdims=True) + jnp.exp2(sink - m)
                    o = _dot(p.astype(BF16), v_pad[e]) / denom
                    out = o if out is None else out + o
                o_ref[rows, lanes] = out.astype(BF16)


def _swa_attn(q, kv, bias, sinks):
    b, s, _ = q.shape
    t = ATTN_BLOCKS * WINDOW
    return pl.pallas_call(
        _swa_attn_kernel,
        grid=(b, s // t),
        in_specs=[
            pl.BlockSpec((None, t, D_MODEL), lambda i, n: (i, n, 0)),
            pl.BlockSpec((None, t, 2 * SWA_KV_WIDTH), lambda i, n: (i, n, 0)),
            pl.BlockSpec((None, WINDOW, 2 * SWA_KV_WIDTH),
                         lambda i, n: (i, jnp.maximum(n * ATTN_BLOCKS - 1, 0), 0)),
            _resident((2, SWA_HEADS, WINDOW, 2 * WINDOW)),
            pl.BlockSpec(memory_space=pltpu.SMEM),
        ],
        out_specs=pl.BlockSpec((None, t, D_MODEL), lambda i, n: (i, n, 0)),
        out_shape=jax.ShapeDtypeStruct((b, s, D_MODEL), BF16),
        compiler_params=_params("parallel", "parallel"),
        name="swa_attn",
    )(q, kv, kv, bias, sinks)


def _dn_in_kernel(x_ref, nw_ref, w_ref, cw_ref, alog_ref, dt_ref, qkv_ref, z_ref, bg_ref, bgt_ref,
                  halo_ref, pbuf_ref):
    tm = ROW_TILE
    gate_cols = 2 * DN_HEADS

    @pl.when(pl.program_id(1) == 0)
    def _():
        halo_ref[...] = jnp.zeros_like(halo_ref)

    h = _rms_norm(x_ref[...], nw_ref[...]).astype(BF16)

    def gate_projection():
        pb = _dot(h, w_ref[:, 4 * DN_WIDTH:])
        beta = _sigmoid(pb)
        t = pb + dt_ref[...]
        softplus = jnp.maximum(t, 0.0) + jnp.log(1.0 + jnp.exp(-jnp.abs(t)))
        g = -jnp.exp(alog_ref[...]) * softplus
        ri = lax.broadcasted_iota(jnp.int32, (DELTA_CHUNK, DELTA_CHUNK), 0)
        ci = lax.broadcasted_iota(jnp.int32, (DELTA_CHUNK, DELTA_CHUNK), 1)
        tri = (ri >= ci).astype(F32)
        col = lax.broadcasted_iota(jnp.int32, (DELTA_CHUNK, gate_cols), 1)
        pad = jnp.zeros((DELTA_CHUNK, LANES - gate_cols), F32)
        for c in range(tm // DELTA_CHUNK):
            rows = slice(c * DELTA_CHUNK, (c + 1) * DELTA_CHUNK)
            gc = jnp.dot(tri, g[rows], preferred_element_type=F32, precision=lax.Precision.HIGHEST)
            gates = jnp.where(col < DN_HEADS, beta[rows], gc)
            bg_ref[rows, :] = gates
            bgt_ref[:, rows] = jnp.concatenate([gates, pad], axis=1).T[:gate_cols, :]

    n_chunks = 3 * DN_WIDTH // CONV_COLS
    z_cols = DN_WIDTH // (n_chunks - 2)
    for c in range(n_chunks):
        c0 = c * CONV_COLS
        cols = slice(c0, c0 + CONV_COLS)
        p = _dot(h, w_ref[:, cols])
        if c < n_chunks - 2:
            z_ref[:, c * z_cols:(c + 1) * z_cols] = _dot(
                h, w_ref[:, 3 * DN_WIDTH + c * z_cols:3 * DN_WIDTH + (c + 1) * z_cols]).astype(BF16)
        elif c == n_chunks - 2:
            gate_projection()
        pbuf_ref[0:SUBLANES, :] = halo_ref[:, cols]
        pbuf_ref[SUBLANES:SUBLANES + tm, :] = p
        halo_ref[:, cols] = p[tm - SUBLANES:, :]
        y = cw_ref[DN_CONV - 1:DN_CONV, cols] * p
        for j in range(DN_CONV - 1):
            start = SUBLANES - (DN_CONV - 1) + j
            y = y + cw_ref[j:j + 1, cols] * pbuf_ref[start:start + tm, :]
        y = y * _sigmoid(y)
        if c0 < 2 * DN_WIDTH:
            scale = DN_HEAD_DIM ** -0.5 if c0 < DN_WIDTH else 1.0
            for hh in range(CONV_COLS // DN_HEAD_DIM):
                t = y[:, hh * DN_HEAD_DIM:(hh + 1) * DN_HEAD_DIM]
                r = lax.rsqrt(jnp.sum(t * t, axis=-1, keepdims=True) + RMS_EPS) * scale
                qkv_ref[:, c0 + hh * DN_HEAD_DIM:c0 + (hh + 1) * DN_HEAD_DIM] = (t * r).astype(BF16)
        else:
            qkv_ref[:, cols] = y.astype(BF16)


def _dn_in(x, norm_w, w_in, layer, conv_w, a_log_row, dt_row):
    b, s, _ = x.shape
    n_in = w_in.shape[-1]
    gate_cols = 2 * DN_HEADS
    tok = lambda i, n: (i, n, 0)
    return pl.pallas_call(
        _dn_in_kernel,
        grid=(b, s // ROW_TILE),
        in_specs=[
            pl.BlockSpec((None, ROW_TILE, D_MODEL), tok),
            _resident((1, D_MODEL)),
            _resident_layer((D_MODEL, n_in), layer),
            _resident((DN_CONV, 3 * DN_WIDTH)),
            _resident((1, gate_cols)),
            _resident((1, gate_cols)),
        ],
        out_specs=[
            pl.BlockSpec((None, ROW_TILE, 3 * DN_WIDTH), tok),
            pl.BlockSpec((None, ROW_TILE, DN_WIDTH), tok),
            pl.BlockSpec((None, ROW_TILE, gate_cols), tok),
            pl.BlockSpec((None, gate_cols, ROW_TILE), lambda i, n: (i, 0, n)),
        ],
        out_shape=[
            jax.ShapeDtypeStruct((b, s, 3 * DN_WIDTH), BF16),
            jax.ShapeDtypeStruct((b, s, DN_WIDTH), BF16),
            jax.ShapeDtypeStruct((b, s, gate_cols), F32),
            jax.ShapeDtypeStruct((b, gate_cols, s), F32),
        ],
        scratch_shapes=[
            pltpu.VMEM((SUBLANES, 3 * DN_WIDTH), F32),
            pltpu.VMEM((SUBLANES + ROW_TILE, CONV_COLS), F32),
        ],
        compiler_params=_params("arbitrary", "arbitrary"),
        name="dn_in",
    )(x, norm_w, w_in, conv_w, a_log_row, dt_row)


def _delta_kernel(qkv_ref, z_ref, bg_ref, bgt_ref, nw_ref, o_ref, state_ref):
    c_len, d = DELTA_CHUNK, DN_HEAD_DIM
    batch = qkv_ref.shape[0]

    @pl.when(pl.program_id(0) == 0)
    def _():
        state_ref[...] = jnp.zeros_like(state_ref)

    ri = lax.broadcasted_iota(jnp.int32, (c_len, c_len), 0)
    ci = lax.broadcasted_iota(jnp.int32, (c_len, c_len), 1)
    causal = ri >= ci
    strict = ri > ci
    eye = (ri == ci).astype(F32)
    sub_blocks = [
        ((ri >> (k + 1)) == (ci >> (k + 1))) & (((ri >> k) & 1) == 1) & (((ci >> k) & 1) == 0)
        for k in range(int(math.log2(c_len)))
    ]
    norm_w = nw_ref[...]

    def operands(i, h):
        q = qkv_ref[i, :, h * d:(h + 1) * d]
        k = qkv_ref[i, :, DN_WIDTH + h * d:DN_WIDTH + (h + 1) * d]
        v = qkv_ref[i, :, 2 * DN_WIDTH + h * d:2 * DN_WIDTH + (h + 1) * d]
        return q, k, v

    def gates(i, h):
        beta_col = bg_ref[i, :, h:h + 1]
        gc_col = bg_ref[i, :, DN_HEADS + h:DN_HEADS + h + 1]
        beta_row = bgt_ref[i, h:h + 1, :]
        gc_row = bgt_ref[i, DN_HEADS + h:DN_HEADS + h + 1, :]
        return beta_col, gc_col, beta_row, gc_row

    problems = [(i, h) for i in range(batch) for h in range(DN_HEADS)]

    attn, neg_lower, inv = {}, {}, {}
    for p in problems:
        q, k, _ = operands(*p)
        beta_col, gc_col, _, gc_row = gates(*p)
        decay = jnp.exp(jnp.where(causal, gc_col - gc_row, -jnp.inf))
        qk = _dot_bt(jnp.concatenate([q, k], axis=0), k)
        attn[p] = (qk[:c_len] * decay).astype(BF16)
        lower = jnp.where(strict, -(qk[c_len:] * (decay * beta_col)), 0.0)
        neg_lower[p] = lower.astype(BF16)
        inv[p] = jnp.where(sub_blocks[0], lower, eye)

    for mask in sub_blocks[1:]:
        inv_b = {p: inv[p].astype(BF16) for p in problems}
        n_t = {p: jnp.where(mask, _dot(neg_lower[p], inv_b[p]), 0.0).astype(BF16) for p in problems}
        inv = {p: inv[p] + _dot(inv_b[p], n_t[p]) for p in problems}

    u, w = {}, {}
    for p in problems:
        _, k, v = operands(*p)
        _, _, beta_row, gc_row = gates(*p)
        u[p] = _dot((inv[p] * beta_row).astype(BF16), v)
        w[p] = _dot((inv[p] * (beta_row * jnp.exp(gc_row))).astype(BF16), k).astype(BF16)

    ws_qs = {}
    for p in problems:
        i, h = p
        q, _, _ = operands(*p)
        ws_qs[p] = _dot(jnp.concatenate([w[p], q], axis=0), state_ref[i * DN_HEADS + h].astype(BF16))

    o_raw, kv = {}, {}
    for p in problems:
        _, k, _ = operands(*p)
        _, gc_col, _, gc_row = gates(*p)
        g_last = gc_row[:, c_len - 1:]
        v_new = u[p] - ws_qs[p][:c_len]
        o_raw[p] = ws_qs[p][c_len:] * jnp.exp(gc_col) + _dot(attn[p], v_new.astype(BF16))
        kv[p] = _dot_at(k, (v_new * jnp.exp(g_last - gc_col)).astype(BF16))

    for p in problems:
        i, h = p
        lanes = slice(h * d, (h + 1) * d)
        g_last = bgt_ref[i, DN_HEADS + h:DN_HEADS + h + 1, c_len - 1:]
        state_ref[i * DN_HEADS + h] = state_ref[i * DN_HEADS + h] * jnp.exp(g_last) + kv[p]
        zf = z_ref[i, :, lanes].astype(F32)
        o = o_raw[p]
        o = o * lax.rsqrt(jnp.mean(o * o, axis=-1, keepdims=True) + RMS_EPS) * norm_w * (zf * _sigmoid(zf))
        o_ref[i, :, lanes] = o.astype(BF16)


def _delta_rule(qkv, z, bg, bgt, norm_w):
    b, s, _ = qkv.shape
    t = DELTA_CHUNK
    tok = lambda n: (0, n, 0)
    return pl.pallas_call(
        _delta_kernel,
        grid=(s // t,),
        in_specs=[
            pl.BlockSpec((b, t, 3 * DN_WIDTH), tok),
            pl.BlockSpec((b, t, DN_WIDTH), tok),
            pl.BlockSpec((b, t, 2 * DN_HEADS), tok),
            pl.BlockSpec((b, 2 * DN_HEADS, t), lambda n: (0, 0, n)),
            _resident((1, DN_HEAD_DIM)),
        ],
        out_specs=pl.BlockSpec((b, t, DN_WIDTH), tok),
        out_shape=jax.ShapeDtypeStruct((b, s, DN_WIDTH), BF16),
        scratch_shapes=[pltpu.VMEM((b * DN_HEADS, DN_HEAD_DIM, DN_HEAD_DIM), F32)],
        compiler_params=_params("arbitrary"),
        name="delta_rule",
    )(qkv, z, bg, bgt, norm_w)


def _row(v):
    return v.reshape(1, -1).astype(F32)


def _gate_row(v):
    return jnp.concatenate([jnp.zeros_like(v), v]).reshape(1, -1).astype(F32)


def kernel(x, norm_mix, norm_mlp, norm_final, dn_w_in, dn_conv_w, dn_a_log, dn_dt_bias, dn_norm_w, dn_w_out,
           swa_w_qkv, swa_b_qkv, swa_sinks, swa_w_out, swa_b_out, rel_bias, mlp_w_up, mlp_w_down):
    b, s, d = x.shape
    depth = norm_mix.shape[0]
    m = b * s
    bias = _t5_bias(rel_bias.astype(F32))
    zero_bias = jnp.zeros((1, d), F32)
    final_w = _row(norm_final)
    dn_w_in, dn_w_out, swa_w_qkv, swa_w_out, mlp_w_up, mlp_w_down = (
        w.astype(BF16) for w in (dn_w_in, dn_w_out, swa_w_qkv, swa_w_out, mlp_w_up, mlp_w_down))
    for i in range(depth):
        j = i // 2
        if i % 2 == 0:
            qkv, z, bg, bgt = _dn_in(x, _row(norm_mix[i]), dn_w_in, j, dn_conv_w[j].astype(F32),
                                     _gate_row(dn_a_log[j]), _gate_row(dn_dt_bias[j]))
            mixed = _delta_rule(qkv, z, bg, bgt, _row(dn_norm_w[j]))
            w_out, b_out = dn_w_out, zero_bias
        else:
            q, kv = _swa_qkv(x.reshape(m, d), _row(norm_mix[i]), swa_w_qkv, j, _row(swa_b_qkv[j]))
            mixed = _swa_attn(q.reshape(b, s, d), kv.reshape(b, s, 2 * SWA_KV_WIDTH), bias,
                              swa_sinks[j].astype(F32))
            w_out, b_out = swa_w_out, _row(swa_b_out[j])
        x = _out_mlp(x.reshape(m, d), mixed.reshape(m, d), w_out, j, b_out, _row(norm_mlp[i]),
                     mlp_w_up, mlp_w_down, i, final_w, final_norm=(i == depth - 1)).reshape(b, s, d)
    return x
```

```python
import functools
import math

import numpy as np
import jax
import jax.numpy as jnp
from jax import lax
from jax.experimental import pallas as pl
from jax.experimental.pallas import tpu as pltpu

F32 = jnp.float32
BF16 = jnp.bfloat16

D_MODEL = 1024
RMS_EPS = 1e-6
DN_HEAD_DIM = 128
DN_HEADS = 8
DN_WIDTH = DN_HEADS * DN_HEAD_DIM
DN_CONV = 4
SWA_HEAD_DIM = 64
SWA_HEADS = 16
SWA_KV_HEADS = 2
SWA_GROUP = SWA_HEADS // SWA_KV_HEADS
SWA_KV_WIDTH = SWA_KV_HEADS * SWA_HEAD_DIM
WINDOW = 128
NUM_BUCKETS = 32
MAX_DISTANCE = 128
LOG2_E = math.log2(math.e)
D_FF = 4 * D_MODEL

V7X_VMEM_BYTES = 64 * 1024 * 1024
VMEM_LIMIT_BYTES = V7X_VMEM_BYTES * 7 // 8
SUBLANES = 8
LANES = 128

ROW_TILE = 512
MLP_TILE = 1024
QKV_TILE = 1024
FF_CHUNK = 1024
CONV_COLS = 512
ATTN_BLOCKS = 4
DELTA_CHUNK = 128


def _resident(shape):
    return pl.BlockSpec(shape, lambda *_: (0,) * len(shape), pipeline_mode=pl.Buffered(1))


def _resident_layer(shape, layer):
    return pl.BlockSpec((None,) + shape, lambda *_: (layer,) + (0,) * len(shape), pipeline_mode=pl.Buffered(1))


def _params(*semantics):
    return pltpu.CompilerParams(dimension_semantics=semantics, vmem_limit_bytes=VMEM_LIMIT_BYTES)


def _rms_norm(x, w):
    return x * lax.rsqrt(jnp.mean(x * x, axis=-1, keepdims=True) + RMS_EPS) * w


def _sigmoid(x):
    return 1.0 / (1.0 + jnp.exp(-x))


def _dot(a, b):
    return jnp.dot(a, b, preferred_element_type=F32)


def _dot_bt(a, b):
    return lax.dot_general(a, b, (((1,), (1,)), ((), ())), preferred_element_type=F32)


def _dot_at(a, b):
    return lax.dot_general(a, b, (((0,), (0,)), ((), ())), preferred_element_type=F32)


def _out_mlp_kernel(x_ref, a_ref, wo_ref, bo_ref, nw_ref, wup_ref, wdn_ref, fw_ref, o_ref, *, final_norm):
    x1 = x_ref[...] + _dot(a_ref[...], wo_ref[...]) + bo_ref[...]
    h = _rms_norm(x1, nw_ref[...]).astype(BF16)
    acc = x1
    for c in range(D_FF // FF_CHUNK):
        cols = slice(c * FF_CHUNK, (c + 1) * FF_CHUNK)
        u = jnp.square(jnp.maximum(_dot(h, wup_ref[:, cols]), 0.0)).astype(BF16)
        acc = acc + _dot(u, wdn_ref[cols, :])
    if final_norm:
        acc = _rms_norm(acc, fw_ref[...])
    o_ref[...] = acc


def _out_mlp(x, a, w_out, mixer_layer, b_out, norm_w, w_up, w_down, layer, final_w, final_norm):
    m = x.shape[0]
    row = lambda i: (i, 0)
    return pl.pallas_call(
        functools.partial(_out_mlp_kernel, final_norm=final_norm),
        grid=(m // MLP_TILE,),
        in_specs=[
            pl.BlockSpec((MLP_TILE, D_MODEL), row),
            pl.BlockSpec((MLP_TILE, D_MODEL), row),
            _resident_layer((D_MODEL, D_MODEL), mixer_layer),
            _resident((1, D_MODEL)),
            _resident((1, D_MODEL)),
            _resident_layer((D_MODEL, D_FF), layer),
            _resident_layer((D_FF, D_MODEL), layer),
            _resident((1, D_MODEL)),
        ],
        out_specs=pl.BlockSpec((MLP_TILE, D_MODEL), row),
        out_shape=jax.ShapeDtypeStruct((m, D_MODEL), F32),
        compiler_params=_params("parallel"),
        name="out_mlp",
    )(x, a, w_out, b_out, norm_w, w_up, w_down, final_w)


def _swa_qkv_kernel(x_ref, nw_ref, w_ref, b_ref, q_ref, kv_ref):
    h = _rms_norm(x_ref[...], nw_ref[...]).astype(BF16)
    y = _dot(h, w_ref[...]) + b_ref[...]
    q_ref[...] = (y[:, :D_MODEL] * (SWA_HEAD_DIM ** -0.5 * LOG2_E)).astype(BF16)
    kv_ref[...] = y[:, D_MODEL:].astype(BF16)


def _swa_qkv(x, norm_w, w_qkv, layer, b_qkv):
    m = x.shape[0]
    n_out = D_MODEL + 2 * SWA_KV_WIDTH
    row = lambda i: (i, 0)
    return pl.pallas_call(
        _swa_qkv_kernel,
        grid=(m // QKV_TILE,),
        in_specs=[
            pl.BlockSpec((QKV_TILE, D_MODEL), row),
            _resident((1, D_MODEL)),
            _resident_layer((D_MODEL, n_out), layer),
            _resident((1, n_out)),
        ],
        out_specs=[
            pl.BlockSpec((QKV_TILE, D_MODEL), row),
            pl.BlockSpec((QKV_TILE, 2 * SWA_KV_WIDTH), row),
        ],
        out_shape=[
            jax.ShapeDtypeStruct((m, D_MODEL), BF16),
            jax.ShapeDtypeStruct((m, 2 * SWA_KV_WIDTH), BF16),
        ],
        compiler_params=_params("parallel"),
        name="swa_qkv",
    )(x, norm_w, w_qkv, b_qkv)


def _t5_bucket_table():
    qi = np.arange(WINDOW)[None, :]
    kj = np.arange(2 * WINDOW)[:, None]
    dist = qi + WINDOW - kj
    n = np.maximum(dist, 0)
    max_exact = NUM_BUCKETS // 2
    ratio = np.log(np.maximum(n, 1).astype(np.float32) / np.float32(max_exact)) / np.float32(
        math.log(MAX_DISTANCE / max_exact))
    large = max_exact + (ratio * np.float32(NUM_BUCKETS - max_exact)).astype(np.int32)
    large = np.minimum(large, NUM_BUCKETS - 1)
    bucket = np.where(n < max_exact, n, large)
    valid = (dist >= 0) & (dist < WINDOW)
    return np.where(valid, bucket, -1).astype(np.int32)


def _t5_bias_kernel(bucket_ref, rel_ref, o_ref):
    head = pl.program_id(0)
    bucket = bucket_ref[...]
    acc = jnp.full(bucket.shape, -jnp.inf, F32)
    for b in range(NUM_BUCKETS):
        acc = jnp.where(bucket == b, rel_ref[b, head] * LOG2_E, acc)
    o_ref[0, 0] = acc
    key = lax.broadcasted_iota(jnp.int32, bucket.shape, 0)
    o_ref[1, 0] = jnp.where(key < WINDOW, -jnp.inf, acc)


def _t5_bias(rel_bias):
    shape = (2 * WINDOW, WINDOW)
    return pl.pallas_call(
        _t5_bias_kernel,
        grid=(SWA_HEADS,),
        in_specs=[
            pl.BlockSpec(shape, lambda h: (0, 0)),
            pl.BlockSpec(memory_space=pltpu.SMEM),
        ],
        out_specs=pl.BlockSpec((2, 1) + shape, lambda h: (0, h, 0, 0)),
        out_shape=jax.ShapeDtypeStruct((2, SWA_HEADS) + shape, F32),
        compiler_params=_params("parallel"),
        name="t5_bias",
    )(jnp.asarray(_t5_bucket_table()), rel_bias)


def _reduce_rows(x, op):
    slabs = [x[r:r + SUBLANES] for r in range(0, x.shape[0], SUBLANES)]
    while len(slabs) > 1:
        slabs = [op(a, b) for a, b in zip(slabs[0::2], slabs[1::2])]
    slab = slabs[0]
    if op is jnp.add:
        return jnp.sum(slab, axis=0, keepdims=True)
    return jnp.max(slab, axis=0, keepdims=True)


def _swa_attn_kernel(q_ref, kvc_ref, kvp_ref, bias_ref, sink_ref, o_ref):
    hd = SWA_HEAD_DIM
    first_table = jnp.where(pl.program_id(1) == 0, 1, 0)
    zeros = jnp.zeros((2 * WINDOW, hd), BF16)
    ones = jnp.ones((2 * SUBLANES, 2 * WINDOW), BF16)

    for blk in range(ATTN_BLOCKS):
        rows = slice(blk * WINDOW, (blk + 1) * WINDOW)
        if blk == 0:
            kv = jnp.concatenate([kvp_ref[...], kvc_ref[:WINDOW, :]], axis=0)
            table = first_table
        else:
            kv = kvc_ref[(blk - 1) * WINDOW:(blk + 1) * WINDOW, :]
            table = 0
        values_t = kv[:, SWA_KV_WIDTH:].astype(F32).T.astype(BF16)
        k_pad = {}
        for g in range(SWA_KV_HEADS):
            k_g = kv[:, g * hd:(g + 1) * hd]
            k_pad[g] = (jnp.concatenate([k_g, zeros], axis=1), jnp.concatenate([zeros, k_g], axis=1))
        heads = range(SWA_HEADS)
        scores = [_dot_bt(k_pad[h // SWA_GROUP][h % 2], q_ref[rows, (h // 2) * 2 * hd:(h // 2 + 1) * 2 * hd])
                  + bias_ref[table, h] for h in heads]
        sinks = [sink_ref[h] * LOG2_E for h in heads]
        maxes = [jnp.maximum(_reduce_rows(scores[h], jnp.maximum), sinks[h]) for h in heads]
        probs = [jnp.exp2(scores[h] - maxes[h]) for h in heads]
        outs = [_dot(jnp.concatenate([values_t[(h // SWA_GROUP) * hd:(h // SWA_GROUP + 1) * hd, :], ones], axis=0),
                     probs[h].astype(BF16)) for h in heads]
        for pair in range(SWA_HEADS // 2):
            halves = [outs[h][:hd] / (outs[h][hd:hd + 1] + jnp.exp2(sinks[h] - maxes[h]))
                      for h in (2 * pair, 2 * pair + 1)]
            o_ref[rows, pair * 2 * hd:(pair + 1) * 2 * hd] = jnp.concatenate(halves, axis=0).T.astype(BF16)


def _swa_attn(q, kv, bias, sinks):
    b, s, _ = q.shape
    t = ATTN_BLOCKS * WINDOW
    return pl.pallas_call(
        _swa_attn_kernel,
        grid=(b, s // t),
        in_specs=[
            pl.BlockSpec((None, t, D_MODEL), lambda i, n: (i, n, 0)),
            pl.BlockSpec((None, t, 2 * SWA_KV_WIDTH), lambda i, n: (i, n, 0)),
            pl.BlockSpec((None, WINDOW, 2 * SWA_KV_WIDTH),
                         lambda i, n: (i, jnp.maximum(n * ATTN_BLOCKS - 1, 0), 0)),
            _resident((2, SWA_HEADS, 2 * WINDOW, WINDOW)),
            pl.BlockSpec(memory_space=pltpu.SMEM),
        ],
        out_specs=pl.BlockSpec((None, t, D_MODEL), lambda i, n: (i, n, 0)),
        out_shape=jax.ShapeDtypeStruct((b, s, D_MODEL), BF16),
        compiler_params=_params("parallel", "parallel"),
        name="swa_attn",
    )(q, kv, kv, bias, sinks)


def _dn_in_kernel(x_ref, nw_ref, w_ref, cw_ref, alog_ref, dt_ref, qkv_ref, z_ref, bg_ref, bgt_ref,
                  halo_ref, pbuf_ref):
    tm = ROW_TILE
    gate_cols = 2 * DN_HEADS

    @pl.when(pl.program_id(1) == 0)
    def _():
        halo_ref[...] = jnp.zeros_like(halo_ref)

    h = _rms_norm(x_ref[...], nw_ref[...]).astype(BF16)

    def gate_projection():
        pb = _dot(h, w_ref[:, 4 * DN_WIDTH:])
        beta = _sigmoid(pb)
        t = pb + dt_ref[...]
        softplus = jnp.maximum(t, 0.0) + jnp.log(1.0 + jnp.exp(-jnp.abs(t)))
        g = -jnp.exp(alog_ref[...]) * softplus
        ri = lax.broadcasted_iota(jnp.int32, (DELTA_CHUNK, DELTA_CHUNK), 0)
        ci = lax.broadcasted_iota(jnp.int32, (DELTA_CHUNK, DELTA_CHUNK), 1)
        tri = (ri >= ci).astype(F32)
        col = lax.broadcasted_iota(jnp.int32, (DELTA_CHUNK, gate_cols), 1)
        pad = jnp.zeros((DELTA_CHUNK, LANES - gate_cols), F32)
        for c in range(tm // DELTA_CHUNK):
            rows = slice(c * DELTA_CHUNK, (c + 1) * DELTA_CHUNK)
            gc = jnp.dot(tri, g[rows], preferred_element_type=F32, precision=lax.Precision.HIGHEST)
            gates = jnp.where(col < DN_HEADS, beta[rows], gc)
            bg_ref[rows, :] = gates
            bgt_ref[:, rows] = jnp.concatenate([gates, pad], axis=1).T[:gate_cols, :]

    n_chunks = 3 * DN_WIDTH // CONV_COLS
    z_cols = DN_WIDTH // (n_chunks - 2)
    for c in range(n_chunks):
        c0 = c * CONV_COLS
        cols = slice(c0, c0 + CONV_COLS)
        p = _dot(h, w_ref[:, cols])
        if c < n_chunks - 2:
            z_ref[:, c * z_cols:(c + 1) * z_cols] = _dot(
                h, w_ref[:, 3 * DN_WIDTH + c * z_cols:3 * DN_WIDTH + (c + 1) * z_cols]).astype(BF16)
        elif c == n_chunks - 2:
            gate_projection()
        pbuf_ref[0:SUBLANES, :] = halo_ref[:, cols]
        pbuf_ref[SUBLANES:SUBLANES + tm, :] = p
        halo_ref[:, cols] = p[tm - SUBLANES:, :]
        y = cw_ref[DN_CONV - 1:DN_CONV, cols] * p
        for j in range(DN_CONV - 1):
            start = SUBLANES - (DN_CONV - 1) + j
            y = y + cw_ref[j:j + 1, cols] * pbuf_ref[start:start + tm, :]
        y = y * _sigmoid(y)
        if c0 < 2 * DN_WIDTH:
            scale = DN_HEAD_DIM ** -0.5 if c0 < DN_WIDTH else 1.0
            for hh in range(CONV_COLS // DN_HEAD_DIM):
                t = y[:, hh * DN_HEAD_DIM:(hh + 1) * DN_HEAD_DIM]
                r = lax.rsqrt(jnp.sum(t * t, axis=-1, keepdims=True) + RMS_EPS) * scale
                qkv_ref[:, c0 + hh * DN_HEAD_DIM:c0 + (hh + 1) * DN_HEAD_DIM] = (t * r).astype(BF16)
        else:
            qkv_ref[:, cols] = y.astype(BF16)


def _dn_in(x, norm_w, w_in, layer, conv_w, a_log_row, dt_row):
    b, s, _ = x.shape
    n_in = w_in.shape[-1]
    gate_cols = 2 * DN_HEADS
    tok = lambda i, n: (i, n, 0)
    return pl.pallas_call(
        _dn_in_kernel,
        grid=(b, s // ROW_TILE),
        in_specs=[
            pl.BlockSpec((None, ROW_TILE, D_MODEL), tok),
            _resident((1, D_MODEL)),
            _resident_layer((D_MODEL, n_in), layer),
            _resident((DN_CONV, 3 * DN_WIDTH)),
            _resident((1, gate_cols)),
            _resident((1, gate_cols)),
        ],
        out_specs=[
            pl.BlockSpec((None, ROW_TILE, 3 * DN_WIDTH), tok),
            pl.BlockSpec((None, ROW_TILE, DN_WIDTH), tok),
            pl.BlockSpec((None, ROW_TILE, gate_cols), tok),
            pl.BlockSpec((None, gate_cols, ROW_TILE), lambda i, n: (i, 0, n)),
        ],
        out_shape=[
            jax.ShapeDtypeStruct((b, s, 3 * DN_WIDTH), BF16),
            jax.ShapeDtypeStruct((b, s, DN_WIDTH), BF16),
            jax.ShapeDtypeStruct((b, s, gate_cols), F32),
            jax.ShapeDtypeStruct((b, gate_cols, s), F32),
        ],
        scratch_shapes=[
            pltpu.VMEM((SUBLANES, 3 * DN_WIDTH), F32),
            pltpu.VMEM((SUBLANES + ROW_TILE, CONV_COLS), F32),
        ],
        compiler_params=_params("arbitrary", "arbitrary"),
        name="dn_in",
    )(x, norm_w, w_in, conv_w, a_log_row, dt_row)


def _delta_kernel(qkv_ref, z_ref, bg_ref, bgt_ref, nw_ref, o_ref, state_ref):
    c_len, d = DELTA_CHUNK, DN_HEAD_DIM
    batch = qkv_ref.shape[0]

    @pl.when(pl.program_id(0) == 0)
    def _():
        state_ref[...] = jnp.zeros_like(state_ref)

    ri = lax.broadcasted_iota(jnp.int32, (c_len, c_len), 0)
    ci = lax.broadcasted_iota(jnp.int32, (c_len, c_len), 1)
    causal = ri >= ci
    strict = ri > ci
    eye = (ri == ci).astype(F32)
    sub_blocks = [
        ((ri >> (k + 1)) == (ci >> (k + 1))) & (((ri >> k) & 1) == 1) & (((ci >> k) & 1) == 0)
        for k in range(int(math.log2(c_len)))
    ]
    norm_w = nw_ref[...]

    def operands(i, h):
        q = qkv_ref[i, :, h * d:(h + 1) * d]
        k = qkv_ref[i, :, DN_WIDTH + h * d:DN_WIDTH + (h + 1) * d]
        v = qkv_ref[i, :, 2 * DN_WIDTH + h * d:2 * DN_WIDTH + (h + 1) * d]
        return q, k, v

    def gates(i, h):
        beta_col = bg_ref[i, :, h:h + 1]
        gc_col = bg_ref[i, :, DN_HEADS + h:DN_HEADS + h + 1]
        beta_row = bgt_ref[i, h:h + 1, :]
        gc_row = bgt_ref[i, DN_HEADS + h:DN_HEADS + h + 1, :]
        return beta_col, gc_col, beta_row, gc_row

    problems = [(i, h) for i in range(batch) for h in range(DN_HEADS)]

    attn, neg_lower, inv = {}, {}, {}
    for p in problems:
        q, k, _ = operands(*p)
        beta_col, gc_col, _, gc_row = gates(*p)
        decay = jnp.exp(jnp.where(causal, gc_col - gc_row, -jnp.inf))
        qk = _dot_bt(jnp.concatenate([q, k], axis=0), k)
        attn[p] = (qk[:c_len] * decay).astype(BF16)
        lower = jnp.where(strict, -(qk[c_len:] * (decay * beta_col)), 0.0)
        neg_lower[p] = lower.astype(BF16)
        inv[p] = jnp.where(sub_blocks[0], lower, eye)

    for mask in sub_blocks[1:]:
        inv_b = {p: inv[p].astype(BF16) for p in problems}
        n_t = {p: jnp.where(mask, _dot(neg_lower[p], inv_b[p]), 0.0).astype(BF16) for p in problems}
        inv = {p: inv[p] + _dot(inv_b[p], n_t[p]) for p in problems}

    u, w = {}, {}
    for p in problems:
        _, k, v = operands(*p)
        _, _, beta_row, gc_row = gates(*p)
        u[p] = _dot((inv[p] * beta_row).astype(BF16), v)
        w[p] = _dot((inv[p] * (beta_row * jnp.exp(gc_row))).astype(BF16), k).astype(BF16)

    ws_qs = {}
    for p in problems:
        i, h = p
        q, _, _ = operands(*p)
        ws_qs[p] = _dot(jnp.concatenate([w[p], q], axis=0), state_ref[i * DN_HEADS + h].astype(BF16))

    o_raw, kv = {}, {}
    for p in problems:
        _, k, _ = operands(*p)
        _, gc_col, _, gc_row = gates(*p)
        g_last = gc_row[:, c_len - 1:]
        v_new = u[p] - ws_qs[p][:c_len]
        o_raw[p] = ws_qs[p][c_len:] * jnp.exp(gc_col) + _dot(attn[p], v_new.astype(BF16))
        kv[p] = _dot_at(k, (v_new * jnp.exp(g_last - gc_col)).astype(BF16))

    for p in problems:
        i, h = p
        lanes = slice(h * d, (h + 1) * d)
        g_last = bgt_ref[i, DN_HEADS + h:DN_HEADS + h + 1, c_len - 1:]
        state_ref[i * DN_HEADS + h] = state_ref[i * DN_HEADS + h] * jnp.exp(g_last) + kv[p]
        zf = z_ref[i, :, lanes].astype(F32)
        o = o_raw[p]
        o = o * lax.rsqrt(jnp.mean(o * o, axis=-1, keepdims=True) + RMS_EPS) * norm_w * (zf * _sigmoid(zf))
        o_ref[i, :, lanes] = o.astype(BF16)


def _delta_rule(qkv, z, bg, bgt, norm_w):
    b, s, _ = qkv.shape
    t = DELTA_CHUNK
    tok = lambda n: (0, n, 0)
    return pl.pallas_call(
        _delta_kernel,
        grid=(s // t,),
        in_specs=[
            pl.BlockSpec((b, t, 3 * DN_WIDTH), tok),
            pl.BlockSpec((b, t, DN_WIDTH), tok),
            pl.BlockSpec((b, t, 2 * DN_HEADS), tok),
            pl.BlockSpec((b, 2 * DN_HEADS, t), lambda n: (0, 0, n)),
            _resident((1, DN_HEAD_DIM)),
        ],
        out_specs=pl.BlockSpec((b, t, DN_WIDTH), tok),
        out_shape=jax.ShapeDtypeStruct((b, s, DN_WIDTH), BF16),
        scratch_shapes=[pltpu.VMEM((b * DN_HEADS, DN_HEAD_DIM, DN_HEAD_DIM), F32)],
        compiler_params=_params("arbitrary"),
        name="delta_rule",
    )(qkv, z, bg, bgt, norm_w)


def _row(v):
    return v.reshape(1, -1).astype(F32)


def _gate_row(v):
    return jnp.concatenate([jnp.zeros_like(v), v]).reshape(1, -1).astype(F32)


def kernel(x, norm_mix, norm_mlp, norm_final, dn_w_in, dn_conv_w, dn_a_log, dn_dt_bias, dn_norm_w, dn_w_out,
           swa_w_qkv, swa_b_qkv, swa_sinks, swa_w_out, swa_b_out, rel_bias, mlp_w_up, mlp_w_down):
    b, s, d = x.shape
    depth = norm_mix.shape[0]
    m = b * s
    bias = _t5_bias(rel_bias.astype(F32))
    zero_bias = jnp.zeros((1, d), F32)
    final_w = _row(norm_final)
    dn_w_in, dn_w_out, swa_w_qkv, swa_w_out, mlp_w_up, mlp_w_down = (
        w.astype(BF16) for w in (dn_w_in, dn_w_out, swa_w_qkv, swa_w_out, mlp_w_up, mlp_w_down))
    for i in range(depth):
        j = i // 2
        if i % 2 == 0:
            qkv, z, bg, bgt = _dn_in(x, _row(norm_mix[i]), dn_w_in, j, dn_conv_w[j].astype(F32),
                                     _gate_row(dn_a_log[j]), _gate_row(dn_dt_bias[j]))
            mixed = _delta_rule(qkv, z, bg, bgt, _row(dn_norm_w[j]))
            w_out, b_out = dn_w_out, zero_bias
        else:
            q, kv = _swa_qkv(x.reshape(m, d), _row(norm_mix[i]), swa_w_qkv, j, _row(swa_b_qkv[j]))
            mixed = _swa_attn(q.reshape(b, s, d), kv.reshape(b, s, 2 * SWA_KV_WIDTH), bias,
                              swa_sinks[j].astype(F32))
            w_out, b_out = swa_w_out, _row(swa_b_out[j])
        x = _out_mlp(x.reshape(m, d), mixed.reshape(m, d), w_out, j, b_out, _row(norm_mlp[i]),
                     mlp_w_up, mlp_w_down, i, final_w, final_norm=(i == depth - 1)).reshape(b, s, d)
    return x
```

```python
import functools
import math

import numpy as np
import jax
import jax.numpy as jnp
from jax import lax
from jax.experimental import pallas as pl
from jax.experimental.pallas import tpu as pltpu

F32 = jnp.float32
BF16 = jnp.bfloat16

D_MODEL = 1024
RMS_EPS = 1e-6
DN_HEAD_DIM = 128
DN_HEADS = 8
DN_WIDTH = DN_HEADS * DN_HEAD_DIM
DN_CONV = 4
SWA_HEAD_DIM = 64
SWA_HEADS = 16
SWA_KV_HEADS = 2
SWA_GROUP = SWA_HEADS // SWA_KV_HEADS
SWA_KV_WIDTH = SWA_KV_HEADS * SWA_HEAD_DIM
WINDOW = 128
NUM_BUCKETS = 32
MAX_DISTANCE = 128
LOG2_E = math.log2(math.e)
D_FF = 4 * D_MODEL

V7X_VMEM_BYTES = 64 * 1024 * 1024
VMEM_LIMIT_BYTES = V7X_VMEM_BYTES * 7 // 8
SUBLANES = 8
LANES = 128

ROW_TILE = 512
MLP_TILE = 1024
QKV_TILE = 1024
FF_CHUNK = 1024
CONV_COLS = 512
ATTN_BLOCKS = 8
DELTA_CHUNK = 128


def _resident(shape):
    return pl.BlockSpec(shape, lambda *_: (0,) * len(shape), pipeline_mode=pl.Buffered(1))


def _resident_layer(shape, layer):
    return pl.BlockSpec((None,) + shape, lambda *_: (layer,) + (0,) * len(shape), pipeline_mode=pl.Buffered(1))


def _params(*semantics):
    return pltpu.CompilerParams(dimension_semantics=semantics, vmem_limit_bytes=VMEM_LIMIT_BYTES)


def _rms_norm(x, w):
    return x * lax.rsqrt(jnp.mean(x * x, axis=-1, keepdims=True) + RMS_EPS) * w


def _sigmoid(x):
    return 1.0 / (1.0 + jnp.exp(-x))


def _dot(a, b):
    return jnp.dot(a, b, preferred_element_type=F32)


def _dot_bt(a, b):
    return lax.dot_general(a, b, (((1,), (1,)), ((), ())), preferred_element_type=F32)


def _dot_at(a, b):
    return lax.dot_general(a, b, (((0,), (0,)), ((), ())), preferred_element_type=F32)


def _out_mlp_kernel(x_ref, a_ref, wo_ref, bo_ref, nw_ref, wup_ref, wdn_ref, fw_ref, o_ref, *, final_norm):
    x1 = x_ref[...] + _dot(a_ref[...], wo_ref[...]) + bo_ref[...]
    h = _rms_norm(x1, nw_ref[...]).astype(BF16)
    acc = x1
    for c in range(D_FF // FF_CHUNK):
        cols = slice(c * FF_CHUNK, (c + 1) * FF_CHUNK)
        u = jnp.square(jnp.maximum(_dot(h, wup_ref[:, cols]), 0.0)).astype(BF16)
        acc = acc + _dot(u, wdn_ref[cols, :])
    if final_norm:
        acc = _rms_norm(acc, fw_ref[...])
    o_ref[...] = acc


def _out_mlp(x, a, w_out, mixer_layer, b_out, norm_w, w_up, w_down, layer, final_w, final_norm):
    m = x.shape[0]
    row = lambda i: (i, 0)
    return pl.pallas_call(
        functools.partial(_out_mlp_kernel, final_norm=final_norm),
        grid=(m // MLP_TILE,),
        in_specs=[
            pl.BlockSpec((MLP_TILE, D_MODEL), row),
            pl.BlockSpec((MLP_TILE, D_MODEL), row),
            _resident_layer((D_MODEL, D_MODEL), mixer_layer),
            _resident((1, D_MODEL)),
            _resident((1, D_MODEL)),
            _resident_layer((D_MODEL, D_FF), layer),
            _resident_layer((D_FF, D_MODEL), layer),
            _resident((1, D_MODEL)),
        ],
        out_specs=pl.BlockSpec((MLP_TILE, D_MODEL), row),
        out_shape=jax.ShapeDtypeStruct((m, D_MODEL), F32),
        compiler_params=_params("parallel"),
        name="out_mlp",
    )(x, a, w_out, b_out, norm_w, w_up, w_down, final_w)


def _swa_qkv_kernel(x_ref, nw_ref, w_ref, b_ref, q_ref, kv_ref):
    h = _rms_norm(x_ref[...], nw_ref[...]).astype(BF16)
    y = _dot(h, w_ref[...]) + b_ref[...]
    q_ref[...] = (y[:, :D_MODEL] * (SWA_HEAD_DIM ** -0.5 * LOG2_E)).astype(BF16)
    kv_ref[...] = y[:, D_MODEL:].astype(BF16)


def _swa_qkv(x, norm_w, w_qkv, layer, b_qkv):
    m = x.shape[0]
    n_out = D_MODEL + 2 * SWA_KV_WIDTH
    row = lambda i: (i, 0)
    return pl.pallas_call(
        _swa_qkv_kernel,
        grid=(m // QKV_TILE,),
        in_specs=[
            pl.BlockSpec((QKV_TILE, D_MODEL), row),
            _resident((1, D_MODEL)),
            _resident_layer((D_MODEL, n_out), layer),
            _resident((1, n_out)),
        ],
        out_specs=[
            pl.BlockSpec((QKV_TILE, D_MODEL), row),
            pl.BlockSpec((QKV_TILE, 2 * SWA_KV_WIDTH), row),
        ],
        out_shape=[
            jax.ShapeDtypeStruct((m, D_MODEL), BF16),
            jax.ShapeDtypeStruct((m, 2 * SWA_KV_WIDTH), BF16),
        ],
        compiler_params=_params("parallel"),
        name="swa_qkv",
    )(x, norm_w, w_qkv, b_qkv)


def _t5_bucket_table():
    qi = np.arange(WINDOW)[None, :]
    kj = np.arange(2 * WINDOW)[:, None]
    dist = qi + WINDOW - kj
    n = np.maximum(dist, 0)
    max_exact = NUM_BUCKETS // 2
    ratio = np.log(np.maximum(n, 1).astype(np.float32) / np.float32(max_exact)) / np.float32(
        math.log(MAX_DISTANCE / max_exact))
    large = max_exact + (ratio * np.float32(NUM_BUCKETS - max_exact)).astype(np.int32)
    large = np.minimum(large, NUM_BUCKETS - 1)
    bucket = np.where(n < max_exact, n, large)
    valid = (dist >= 0) & (dist < WINDOW)
    return np.where(valid, bucket, -1).astype(np.int32)


def _t5_bias_kernel(bucket_ref, rel_ref, o_ref):
    head = pl.program_id(0)
    bucket = bucket_ref[...]
    acc = jnp.full(bucket.shape, -jnp.inf, F32)
    for b in range(NUM_BUCKETS):
        acc = jnp.where(bucket == b, rel_ref[b, head] * LOG2_E, acc)
    o_ref[0, 0] = acc
    key = lax.broadcasted_iota(jnp.int32, bucket.shape, 0)
    o_ref[1, 0] = jnp.where(key < WINDOW, -jnp.inf, acc)


def _t5_bias(rel_bias):
    shape = (2 * WINDOW, WINDOW)
    return pl.pallas_call(
        _t5_bias_kernel,
        grid=(SWA_HEADS,),
        in_specs=[
            pl.BlockSpec(shape, lambda h: (0, 0)),
            pl.BlockSpec(memory_space=pltpu.SMEM),
        ],
        out_specs=pl.BlockSpec((2, 1) + shape, lambda h: (0, h, 0, 0)),
        out_shape=jax.ShapeDtypeStruct((2, SWA_HEADS) + shape, F32),
        compiler_params=_params("parallel"),
        name="t5_bias",
    )(jnp.asarray(_t5_bucket_table()), rel_bias)


def _reduce_rows(x, op):
    slabs = [x[r:r + SUBLANES] for r in range(0, x.shape[0], SUBLANES)]
    while len(slabs) > 1:
        slabs = [op(a, b) for a, b in zip(slabs[0::2], slabs[1::2])]
    slab = slabs[0]
    if op is jnp.add:
        return jnp.sum(slab, axis=0, keepdims=True)
    return jnp.max(slab, axis=0, keepdims=True)


def _swa_attn_kernel(q_ref, kvc_ref, kvp_ref, bias_ref, sink_ref, o_ref):
    hd = SWA_HEAD_DIM
    first_table = jnp.where(pl.program_id(1) == 0, 1, 0)
    zeros = jnp.zeros((2 * WINDOW, hd), BF16)
    ones = jnp.ones((2 * SUBLANES, 2 * WINDOW), BF16)

    for blk in range(ATTN_BLOCKS):
        rows = slice(blk * WINDOW, (blk + 1) * WINDOW)
        if blk == 0:
            kv = jnp.concatenate([kvp_ref[...], kvc_ref[:WINDOW, :]], axis=0)
            table = first_table
        else:
            kv = kvc_ref[(blk - 1) * WINDOW:(blk + 1) * WINDOW, :]
            table = 0
        values_t = kv[:, SWA_KV_WIDTH:].astype(F32).T.astype(BF16)
        k_pad = {}
        for g in range(SWA_KV_HEADS):
            k_g = kv[:, g * hd:(g + 1) * hd]
            k_pad[g] = (jnp.concatenate([k_g, zeros], axis=1), jnp.concatenate([zeros, k_g], axis=1))
        heads = range(SWA_HEADS)
        scores = [_dot_bt(k_pad[h // SWA_GROUP][h % 2], q_ref[rows, (h // 2) * 2 * hd:(h // 2 + 1) * 2 * hd])
                  + bias_ref[table, h] for h in heads]
        sinks = [sink_ref[h] * LOG2_E for h in heads]
        maxes = [jnp.maximum(_reduce_rows(scores[h], jnp.maximum), sinks[h]) for h in heads]
        probs = [jnp.exp2(scores[h] - maxes[h]) for h in heads]
        outs = [_dot(jnp.concatenate([values_t[(h // SWA_GROUP) * hd:(h // SWA_GROUP + 1) * hd, :], ones], axis=0),
                     probs[h].astype(BF16)) for h in heads]
        for pair in range(SWA_HEADS // 2):
            halves = [outs[h][:hd] / (outs[h][hd:hd + 1] + jnp.exp2(sinks[h] - maxes[h]))
                      for h in (2 * pair, 2 * pair + 1)]
            o_ref[rows, pair * 2 * hd:(pair + 1) * 2 * hd] = jnp.concatenate(halves, axis=0).T.astype(BF16)


def _swa_attn(q, kv, bias, sinks):
    b, s, _ = q.shape
    t = ATTN_BLOCKS * WINDOW
    return pl.pallas_call(
        _swa_attn_kernel,
        grid=(b, s // t),
        in_specs=[
            pl.BlockSpec((None, t, D_MODEL), lambda i, n: (i, n, 0)),
            pl.BlockSpec((None, t, 2 * SWA_KV_WIDTH), lambda i, n: (i, n, 0)),
            pl.BlockSpec((None, WINDOW, 2 * SWA_KV_WIDTH),
                         lambda i, n: (i, jnp.maximum(n * ATTN_BLOCKS - 1, 0), 0)),
            _resident((2, SWA_HEADS, 2 * WINDOW, WINDOW)),
            pl.BlockSpec(memory_space=pltpu.SMEM),
        ],
        out_specs=pl.BlockSpec((None, t, D_MODEL), lambda i, n: (i, n, 0)),
        out_shape=jax.ShapeDtypeStruct((b, s, D_MODEL), BF16),
        compiler_params=_params("parallel", "parallel"),
        name="swa_attn",
    )(q, kv, kv, bias, sinks)


def _dn_in_kernel(x_ref, nw_ref, w_ref, cw_ref, alog_ref, dt_ref, qkv_ref, z_ref, bg_ref, bgt_ref,
                  halo_ref, pbuf_ref):
    tm = ROW_TILE
    gate_cols = 2 * DN_HEADS

    @pl.when(pl.program_id(1) == 0)
    def _():
        halo_ref[...] = jnp.zeros_like(halo_ref)

    h = _rms_norm(x_ref[...], nw_ref[...]).astype(BF16)

    def gate_projection():
        pb = _dot(h, w_ref[:, 4 * DN_WIDTH:])
        beta = _sigmoid(pb)
        t = pb + dt_ref[...]
        softplus = jnp.maximum(t, 0.0) + jnp.log(1.0 + jnp.exp(-jnp.abs(t)))
        g = -jnp.exp(alog_ref[...]) * softplus
        ri = lax.broadcasted_iota(jnp.int32, (DELTA_CHUNK, DELTA_CHUNK), 0)
        ci = lax.broadcasted_iota(jnp.int32, (DELTA_CHUNK, DELTA_CHUNK), 1)
        tri = (ri >= ci).astype(F32)
        col = lax.broadcasted_iota(jnp.int32, (DELTA_CHUNK, gate_cols), 1)
        pad = jnp.zeros((DELTA_CHUNK, LANES - gate_cols), F32)
        for c in range(tm // DELTA_CHUNK):
            rows = slice(c * DELTA_CHUNK, (c + 1) * DELTA_CHUNK)
            gc = jnp.dot(tri, g[rows], preferred_element_type=F32, precision=lax.Precision.HIGHEST)
            gates = jnp.where(col < DN_HEADS, beta[rows], gc)
            bg_ref[rows, :] = gates
            bgt_ref[:, rows] = jnp.concatenate([gates, pad], axis=1).T[:gate_cols, :]

    n_chunks = 3 * DN_WIDTH // CONV_COLS
    z_cols = DN_WIDTH // (n_chunks - 2)
    for c in range(n_chunks):
        c0 = c * CONV_COLS
        cols = slice(c0, c0 + CONV_COLS)
        p = _dot(h, w_ref[:, cols])
        if c < n_chunks - 2:
            z_ref[:, c * z_cols:(c + 1) * z_cols] = _dot(
                h, w_ref[:, 3 * DN_WIDTH + c * z_cols:3 * DN_WIDTH + (c + 1) * z_cols]).astype(BF16)
        elif c == n_chunks - 2:
            gate_projection()
        pbuf_ref[0:SUBLANES, :] = halo_ref[:, cols]
        pbuf_ref[SUBLANES:SUBLANES + tm, :] = p
        halo_ref[:, cols] = p[tm - SUBLANES:, :]
        y = cw_ref[DN_CONV - 1:DN_CONV, cols] * p
        for j in range(DN_CONV - 1):
            start = SUBLANES - (DN_CONV - 1) + j
            y = y + cw_ref[j:j + 1, cols] * pbuf_ref[start:start + tm, :]
        y = y * _sigmoid(y)
        if c0 < 2 * DN_WIDTH:
            scale = DN_HEAD_DIM ** -0.5 if c0 < DN_WIDTH else 1.0
            for hh in range(CONV_COLS // DN_HEAD_DIM):
                t = y[:, hh * DN_HEAD_DIM:(hh + 1) * DN_HEAD_DIM]
                r = lax.rsqrt(jnp.sum(t * t, axis=-1, keepdims=True) + RMS_EPS) * scale
                qkv_ref[:, c0 + hh * DN_HEAD_DIM:c0 + (hh + 1) * DN_HEAD_DIM] = (t * r).astype(BF16)
        else:
            qkv_ref[:, cols] = y.astype(BF16)


def _dn_in(x, norm_w, w_in, layer, conv_w, a_log_row, dt_row):
    b, s, _ = x.shape
    n_in = w_in.shape[-1]
    gate_cols = 2 * DN_HEADS
    tok = lambda i, n: (i, n, 0)
    return pl.pallas_call(
        _dn_in_kernel,
        grid=(b, s // ROW_TILE),
        in_specs=[
            pl.BlockSpec((None, ROW_TILE, D_MODEL), tok),
            _resident((1, D_MODEL)),
            _resident_layer((D_MODEL, n_in), layer),
            _resident((DN_CONV, 3 * DN_WIDTH)),
            _resident((1, gate_cols)),
            _resident((1, gate_cols)),
        ],
        out_specs=[
            pl.BlockSpec((None, ROW_TILE, 3 * DN_WIDTH), tok),
            pl.BlockSpec((None, ROW_TILE, DN_WIDTH), tok),
            pl.BlockSpec((None, ROW_TILE, gate_cols), tok),
            pl.BlockSpec((None, gate_cols, ROW_TILE), lambda i, n: (i, 0, n)),
        ],
        out_shape=[
            jax.ShapeDtypeStruct((b, s, 3 * DN_WIDTH), BF16),
            jax.ShapeDtypeStruct((b, s, DN_WIDTH), BF16),
            jax.ShapeDtypeStruct((b, s, gate_cols), F32),
            jax.ShapeDtypeStruct((b, gate_cols, s), F32),
        ],
        scratch_shapes=[
            pltpu.VMEM((SUBLANES, 3 * DN_WIDTH), F32),
            pltpu.VMEM((SUBLANES + ROW_TILE, CONV_COLS), F32),
        ],
        compiler_params=_params("arbitrary", "arbitrary"),
        name="dn_in",
    )(x, norm_w, w_in, conv_w, a_log_row, dt_row)


def _delta_kernel(qkv_ref, z_ref, bg_ref, bgt_ref, nw_ref, o_ref, state_ref):
    c_len, d = DELTA_CHUNK, DN_HEAD_DIM
    batch = qkv_ref.shape[0]

    @pl.when(pl.program_id(0) == 0)
    def _():
        state_ref[...] = jnp.zeros_like(state_ref)

    ri = lax.broadcasted_iota(jnp.int32, (c_len, c_len), 0)
    ci = lax.broadcasted_iota(jnp.int32, (c_len, c_len), 1)
    causal = ri >= ci
    strict = ri > ci
    eye = (ri == ci).astype(F32)
    sub_blocks = [
        ((ri >> (k + 1)) == (ci >> (k + 1))) & (((ri >> k) & 1) == 1) & (((ci >> k) & 1) == 0)
        for k in range(int(math.log2(c_len)))
    ]
    norm_w = nw_ref[...]

    def operands(i, h):
        q = qkv_ref[i, :, h * d:(h + 1) * d]
        k = qkv_ref[i, :, DN_WIDTH + h * d:DN_WIDTH + (h + 1) * d]
        v = qkv_ref[i, :, 2 * DN_WIDTH + h * d:2 * DN_WIDTH + (h + 1) * d]
        return q, k, v

    def gates(i, h):
        beta_col = bg_ref[i, :, h:h + 1]
        gc_col = bg_ref[i, :, DN_HEADS + h:DN_HEADS + h + 1]
        beta_row = bgt_ref[i, h:h + 1, :]
        gc_row = bgt_ref[i, DN_HEADS + h:DN_HEADS + h + 1, :]
        return beta_col, gc_col, beta_row, gc_row

    problems = [(i, h) for i in range(batch) for h in range(DN_HEADS)]

    attn, neg_lower, inv = {}, {}, {}
    for p in problems:
        q, k, _ = operands(*p)
        beta_col, gc_col, _, gc_row = gates(*p)
        decay = jnp.exp(jnp.where(causal, gc_col - gc_row, -jnp.inf))
        qk = _dot_bt(jnp.concatenate([q, k], axis=0), k)
        attn[p] = (qk[:c_len] * decay).astype(BF16)
        lower = jnp.where(strict, -(qk[c_len:] * (decay * beta_col)), 0.0)
        neg_lower[p] = lower.astype(BF16)
        inv[p] = jnp.where(sub_blocks[0], lower, eye)

    for mask in sub_blocks[1:]:
        inv_b = {p: inv[p].astype(BF16) for p in problems}
        n_t = {p: jnp.where(mask, _dot(neg_lower[p], inv_b[p]), 0.0).astype(BF16) for p in problems}
        inv = {p: inv[p] + _dot(inv_b[p], n_t[p]) for p in problems}

    u, w = {}, {}
    for p in problems:
        _, k, v = operands(*p)
        _, _, beta_row, gc_row = gates(*p)
        u[p] = _dot((inv[p] * beta_row).astype(BF16), v)
        w[p] = _dot((inv[p] * (beta_row * jnp.exp(gc_row))).astype(BF16), k).astype(BF16)

    ws_qs = {}
    for p in problems:
        i, h = p
        q, _, _ = operands(*p)
        ws_qs[p] = _dot(jnp.concatenate([w[p], q], axis=0), state_ref[i * DN_HEADS + h].astype(BF16))

    o_raw, kv = {}, {}
    for p in problems:
        _, k, _ = operands(*p)
        _, gc_col, _, gc_row = gates(*p)
        g_last = gc_row[:, c_len - 1:]
        v_new = u[p] - ws_qs[p][:c_len]
        o_raw[p] = ws_qs[p][c_len:] * jnp.exp(gc_col) + _dot(attn[p], v_new.astype(BF16))
        kv[p] = _dot_at(k, (v_new * jnp.exp(g_last - gc_col)).astype(BF16))

    for p in problems:
        i, h = p
        lanes = slice(h * d, (h + 1) * d)
        g_last = bgt_ref[i, DN_HEADS + h:DN_HEADS + h + 1, c_len - 1:]
        state_ref[i * DN_HEADS + h] = state_ref[i * DN_HEADS + h] * jnp.exp(g_last) + kv[p]
        zf = z_ref[i, :, lanes].astype(F32)
        o = o_raw[p]
        o = o * lax.rsqrt(jnp.mean(o * o, axis=-1, keepdims=True) + RMS_EPS) * norm_w * (zf * _sigmoid(zf))
        o_ref[i, :, lanes] = o.astype(BF16)


def _delta_rule(qkv, z, bg, bgt, norm_w):
    b, s, _ = qkv.shape
    t = DELTA_CHUNK
    tok = lambda n: (0, n, 0)
    return pl.pallas_call(
        _delta_kernel,
        grid=(s // t,),
        in_specs=[
            pl.BlockSpec((b, t, 3 * DN_WIDTH), tok),
            pl.BlockSpec((b, t, DN_WIDTH), tok),
            pl.BlockSpec((b, t, 2 * DN_HEADS), tok),
            pl.BlockSpec((b, 2 * DN_HEADS, t), lambda n: (0, 0, n)),
            _resident((1, DN_HEAD_DIM)),
        ],
        out_specs=pl.BlockSpec((b, t, DN_WIDTH), tok),
        out_shape=jax.ShapeDtypeStruct((b, s, DN_WIDTH), BF16),
        scratch_shapes=[pltpu.VMEM((b * DN_HEADS, DN_HEAD_DIM, DN_HEAD_DIM), F32)],
        compiler_params=_params("arbitrary"),
        name="delta_rule",
    )(qkv, z, bg, bgt, norm_w)


def _row(v):
    return v.reshape(1, -1).astype(F32)


def _gate_row(v):
    return jnp.concatenate([jnp.zeros_like(v), v]).reshape(1, -1).astype(F32)


def kernel(x, norm_mix, norm_mlp, norm_final, dn_w_in, dn_conv_w, dn_a_log, dn_dt_bias, dn_norm_w, dn_w_out,
           swa_w_qkv, swa_b_qkv, swa_sinks, swa_w_out, swa_b_out, rel_bias, mlp_w_up, mlp_w_down):
    b, s, d = x.shape
    depth = norm_mix.shape[0]
    m = b * s
    bias = _t5_bias(rel_bias.astype(F32))
    zero_bias = jnp.zeros((1, d), F32)
    final_w = _row(norm_final)
    dn_w_in, dn_w_out, swa_w_qkv, swa_w_out, mlp_w_up, mlp_w_down = (
        w.astype(BF16) for w in (dn_w_in, dn_w_out, swa_w_qkv, swa_w_out, mlp_w_up, mlp_w_down))
    for i in range(depth):
        j = i // 2
        if i % 2 == 0:
            qkv, z, bg, bgt = _dn_in(x, _row(norm_mix[i]), dn_w_in, j, dn_conv_w[j].astype(F32),
                                     _gate_row(dn_a_log[j]), _gate_row(dn_dt_bias[j]))
            mixed = _delta_rule(qkv, z, bg, bgt, _row(dn_norm_w[j]))
            w_out, b_out = dn_w_out, zero_bias
        else:
            q, kv = _swa_qkv(x.reshape(m, d), _row(norm_mix[i]), swa_w_qkv, j, _row(swa_b_qkv[j]))
            mixed = _swa_attn(q.reshape(b, s, d), kv.reshape(b, s, 2 * SWA_KV_WIDTH), bias,
                              swa_sinks[j].astype(F32))
            w_out, b_out = swa_w_out, _row(swa_b_out[j])
        x = _out_mlp(x.reshape(m, d), mixed.reshape(m, d), w_out, j, b_out, _row(norm_mlp[i]),
                     mlp_w_up, mlp_w_down, i, final_w, final_norm=(i == depth - 1)).reshape(b, s, d)
    return x
```
